```python
import jax, jax.numpy as jnp
from jax import lax
import numpy as np

D_MODEL = 1024
BATCH = 2
SEQ = 8192
DEPTH = 2
DEC_BATCH = 32
DEC_SEQ = 4
PAST_LEN = 8192
PAGE_SIZE = 128

N_HEADS = 16
HEAD_DIM = D_MODEL // N_HEADS
MIX_WIDTH = N_HEADS * HEAD_DIM
ROT_DIM = HEAD_DIM // 4
ROPE_THETA = 500000.0
NORM_EPS = 1e-6
MOBA_BLOCK = 256
MOBA_TOPK = 3
NSA_KV_GROUPS = 4
NSA_REP = N_HEADS // NSA_KV_GROUPS
KV_WIDTH = NSA_KV_GROUPS * HEAD_DIM
CMP_LEN = 32
CMP_STRIDE = 16
CMP_HIDDEN = 2 * HEAD_DIM
SEL_BLOCK = 64
SEL_TOPN = 16
WINDOW = 512
Q_CHUNK = 32
N_MOBA_LAYERS = (DEPTH + 1) // 2
N_NSA_LAYERS = DEPTH // 2
NSA_IN_WIDTH = MIX_WIDTH + 6 * KV_WIDTH + 3 * N_HEADS + MIX_WIDTH

kernel_name = 'hybrid_moba_nsa_decode_step'


def rms_norm(x, g):
    xf = x.astype(jnp.float32)
    y = xf * lax.rsqrt(jnp.mean(xf * xf, axis=-1, keepdims=True) + NORM_EPS)
    return (y * g.astype(jnp.float32)).astype(x.dtype)


def partial_rope(x, pos):
    half = ROT_DIM // 2
    inv_freq = jnp.float32(ROPE_THETA) ** (-jnp.arange(half, dtype=jnp.float32) / half)
    ang = pos.astype(jnp.float32)[:, None] * inv_freq[None, :]
    cos = jnp.cos(ang)[:, None, :]
    sin = jnp.sin(ang)[:, None, :]
    xr = x[..., :ROT_DIM].astype(jnp.float32)
    x1, x2 = xr[..., :half], xr[..., half:]
    rot = jnp.concatenate([x1 * cos - x2 * sin, x2 * cos + x1 * sin], axis=-1)
    return jnp.concatenate([rot.astype(x.dtype), x[..., ROT_DIM:]], axis=-1)


def masked_softmax(s, mask):
    s = jnp.where(mask, s.astype(jnp.float32), -jnp.inf)
    m = jnp.max(s, axis=-1, keepdims=True)
    m = jnp.where(jnp.isfinite(m), m, 0.0)
    p = jnp.where(mask, jnp.exp(s - m), 0.0)
    return p / jnp.maximum(jnp.sum(p, axis=-1, keepdims=True), 1e-30)


def to_blocks(k, blk):
    t = k.shape[2]
    nb = -(-t // blk)
    k = jnp.pad(k, ((0, 0), (0, 0), (0, nb * blk - t), (0, 0)))
    return k.reshape(k.shape[0], k.shape[1], nb, blk, k.shape[-1])


def map_query_chunks(fn, q, pos):
    s = q.shape[-2]
    nc = s // Q_CHUNK
    qc = jnp.moveaxis(q.reshape(q.shape[:-2] + (nc, Q_CHUNK, q.shape[-1])), -3, 0)
    pc = pos.reshape(nc, Q_CHUNK)
    out = lax.map(lambda a: fn(a[0], a[1]), (qc, pc))
    out = jnp.moveaxis(out, 0, -3)
    return out.reshape(out.shape[:-3] + (s, out.shape[-1]))


def gather_pages(pool, layer, page_table):
    g = pool[layer, page_table]
    return g.reshape((g.shape[0], g.shape[1] * g.shape[2]) + g.shape[3:])


def gated_out(o, z, w_out):
    return (o * jax.nn.silu(z)) @ w_out


def moba_core(q, q_pos, kb, vb, kmean):
    b, h, nq, d = q.shape
    nb = kb.shape[2]
    cur = q_pos // MOBA_BLOCK
    s_blk = jnp.einsum('bhqd,bhnd->bhqn', q, kmean).astype(jnp.float32)
    past = jnp.arange(nb)[None, :] < cur[:, None]
    s_blk = jnp.where(past, s_blk, -jnp.inf)
    kk = min(MOBA_TOPK, nb)
    vals, idx = lax.top_k(s_blk, kk)
    valid = jnp.isfinite(vals)
    idx_all = jnp.concatenate([idx, jnp.broadcast_to(cur[:, None], idx.shape[:-1] + (1,)).astype(idx.dtype)], axis=-1)
    valid_all = jnp.concatenate([valid, jnp.ones(valid.shape[:-1] + (1,), bool)], axis=-1)
    bi = jnp.arange(b)[:, None, None, None]
    hi = jnp.arange(h)[None, :, None, None]
    nj = kk + 1
    kg = kb[bi, hi, idx_all].reshape(b, h, nq, nj * MOBA_BLOCK, d)
    vg = vb[bi, hi, idx_all].reshape(b, h, nq, nj * MOBA_BLOCK, d)
    kpos = (idx_all[..., None] * MOBA_BLOCK + jnp.arange(MOBA_BLOCK)).reshape(b, h, nq, nj * MOBA_BLOCK)
    mask = jnp.repeat(valid_all, MOBA_BLOCK, axis=-1) & (kpos <= q_pos[:, None])
    s = jnp.einsum('bhqd,bhqkd->bhqk', q, kg) * (HEAD_DIM ** -0.5)
    p = masked_softmax(s, mask)
    return jnp.einsum('bhqk,bhqkd->bhqd', p.astype(vg.dtype), vg)


def moba_project(x, pos, norm_g, w_in, q_g, k_g):
    b, t, _ = x.shape
    q, k, v, z = jnp.split(rms_norm(x, norm_g) @ w_in, 4, axis=-1)
    heads = lambda a: a.reshape(b, t, N_HEADS, HEAD_DIM)
    q = partial_rope(rms_norm(heads(q), q_g), pos)
    k = partial_rope(rms_norm(heads(k), k_g), pos)
    return q, k, heads(v), z


def moba_layer(x_p, x_s, kv_past, norm_g, w_in, q_g, k_g, w_out):
    b, s, _ = x_p.shape
    pos_p = jnp.arange(s, dtype=jnp.int32)
    q, k, v, z = moba_project(x_p, pos_p, norm_g, w_in, q_g, k_g)
    kb = to_blocks(k.transpose(0, 2, 1, 3), MOBA_BLOCK)
    vb = to_blocks(v.transpose(0, 2, 1, 3), MOBA_BLOCK)
    kmean = kb.mean(axis=3)
    o = map_query_chunks(lambda qc, pc: moba_core(qc, pc, kb, vb, kmean), q.transpose(0, 2, 1, 3), pos_p)
    y_p = x_p + gated_out(o.transpose(0, 2, 1, 3).reshape(b, s, MIX_WIDTH), z, w_out)
    db, ds, _ = x_s.shape
    p_len = kv_past.shape[1]
    pos_s = p_len + jnp.arange(ds, dtype=jnp.int32)
    qs, ks, vs, zs = moba_project(x_s, pos_s, norm_g, w_in, q_g, k_g)
    kb_s = to_blocks(jnp.concatenate([kv_past[:, :, 0], ks], axis=1).transpose(0, 2, 1, 3), MOBA_BLOCK)
    vb_s = to_blocks(jnp.concatenate([kv_past[:, :, 1], vs], axis=1).transpose(0, 2, 1, 3), MOBA_BLOCK)
    o_s = moba_core(qs.transpose(0, 2, 1, 3), pos_s, kb_s, vb_s, kb_s.mean(axis=3))
    y_s = x_s + gated_out(o_s.transpose(0, 2, 1, 3).reshape(db, ds, MIX_WIDTH), zs, w_out)
    return y_p, y_s, jnp.stack([k, v], axis=2), jnp.stack([ks, vs], axis=2)


def compress(x, pe, w1, b1, w2):
    t = x.shape[2]
    nc = (t - CMP_LEN) // CMP_STRIDE + 1
    idx = jnp.arange(nc)[:, None] * CMP_STRIDE + jnp.arange(CMP_LEN)[None, :]
    blocks = x[:, :, idx] + pe
    flat = blocks.reshape(blocks.shape[:3] + (CMP_LEN * HEAD_DIM,))
    return jax.nn.gelu(flat @ w1 + b1) @ w2


def cmp_to_sel(nc, ns):
    st = jnp.arange(nc)[:, None] * CMP_STRIDE
    j0 = jnp.arange(ns)[None, :] * SEL_BLOCK
    return ((st < j0 + SEL_BLOCK) & (st + CMP_LEN > j0)).astype(jnp.float32)


def nsa_core(q, q_pos, kc, vc, overlap, ksb, vsb, kw, vw, kw_pos):
    b, g, r, nq, d = q.shape
    scale = HEAD_DIM ** -0.5
    nc = kc.shape[2]
    cmp_end = jnp.arange(nc) * CMP_STRIDE + CMP_LEN - 1
    cmask = cmp_end[None, :] <= q_pos[:, None]
    p_c = masked_softmax(jnp.einsum('bgrqd,bgnd->bgrqn', q, kc) * scale, cmask)
    o_c = jnp.einsum('bgrqn,bgnd->bgrqd', p_c.astype(vc.dtype), vc)
    ns = ksb.shape[2]
    imp = p_c.sum(axis=2) @ overlap
    cur = q_pos // SEL_BLOCK
    j = jnp.arange(ns)[None, :]
    forced = (j == 0) | (j == cur[:, None]) | (j == cur[:, None] - 1)
    allowed = j <= cur[:, None]
    imp = jnp.where(forced, jnp.inf, jnp.where(allowed, imp, -jnp.inf))
    kk = min(SEL_TOPN, ns)
    vals, idx = lax.top_k(imp, kk)
    valid = vals > -jnp.inf
    bi = jnp.arange(b)[:, None, None, None]
    gi = jnp.arange(g)[None, :, None, None]
    ks = ksb[bi, gi, idx].reshape(b, g, nq, kk * SEL_BLOCK, d)
    vs = vsb[bi, gi, idx].reshape(b, g, nq, kk * SEL_BLOCK, d)
    kpos = (idx[..., None] * SEL_BLOCK + jnp.arange(SEL_BLOCK)).reshape(b, g, nq, kk * SEL_BLOCK)
    smask = jnp.repeat(valid, SEL_BLOCK, axis=-1) & (kpos <= q_pos[:, None])
    p_s = masked_softmax(jnp.einsum('bgrqd,bgqkd->bgrqk', q, ks) * scale, smask[:, :, None])
    o_s = jnp.einsum('bgrqk,bgqkd->bgrqd', p_s.astype(vs.dtype), vs)
    wmask = (kw_pos[None, :] <= q_pos[:, None]) & (kw_pos[None, :] > q_pos[:, None] - WINDOW) & (kw_pos[None, :] >= 0)
    p_w = masked_softmax(jnp.einsum('bgrqd,bgkd->bgrqk', q, kw) * scale, wmask)
    o_w = jnp.einsum('bgrqk,bgkd->bgrqd', p_w.astype(vw.dtype), vw)
    return jnp.stack([o_c, o_s, o_w], axis=0)


def nsa_project(x, pos, norm_g, w_in, q_g, k_g):
    b, t, _ = x.shape
    splits = np.cumsum([MIX_WIDTH] + [KV_WIDTH] * 6 + [3 * N_HEADS]).tolist()
    q, kc, vc, ks, vs, kw, vw, gt, z = jnp.split(rms_norm(x, norm_g) @ w_in, splits, axis=-1)
    kvh = lambda a: a.reshape(b, t, NSA_KV_GROUPS, HEAD_DIM)
    q = partial_rope(rms_norm(q.reshape(b, t, N_HEADS, HEAD_DIM), q_g), pos)
    kc = partial_rope(kvh(kc), pos)
    ks = partial_rope(rms_norm(kvh(ks), k_g[1]), pos)
    kw = partial_rope(rms_norm(kvh(kw), k_g[2]), pos)
    gates = jax.nn.sigmoid(gt).reshape(b, t, NSA_KV_GROUPS, NSA_REP, 3)
    return q, kc, kvh(vc), ks, kvh(vs), kw, kvh(vw), gates, z


def nsa_keys(kc, vc, ks, vs, k_g, cmp_pe, cmp_w1, cmp_b1, cmp_w2):
    tr = lambda a: a.transpose(0, 2, 1, 3)
    kcmp = rms_norm(compress(tr(kc), cmp_pe[0], cmp_w1[0], cmp_b1[0], cmp_w2[0]), k_g[0])
    vcmp = compress(tr(vc), cmp_pe[1], cmp_w1[1], cmp_b1[1], cmp_w2[1])
    ksb = to_blocks(tr(ks), SEL_BLOCK)
    vsb = to_blocks(tr(vs), SEL_BLOCK)
    return kcmp, vcmp, cmp_to_sel(kcmp.shape[2], ksb.shape[2]), ksb, vsb


def group_q(q):
    b, t = q.shape[:2]
    return q.reshape(b, t, NSA_KV_GROUPS, NSA_REP, HEAD_DIM).transpose(0, 2, 3, 1, 4)


def combine(o, gates):
    b, t = gates.shape[:2]
    return jnp.einsum('cbgrtd,btgrc->btgrd', o, gates).reshape(b, t, MIX_WIDTH)


def nsa_layer(x_p, x_s, kv_past, win_past, norm_g, w_in, q_g, k_g, cmp_pe, cmp_w1, cmp_b1, cmp_w2, w_out):
    b, s, _ = x_p.shape
    pos_p = jnp.arange(s, dtype=jnp.int32)
    q, kc, vc, ks, vs, kw, vw, gates, z = nsa_project(x_p, pos_p, norm_g, w_in, q_g, k_g)
    kcmp, vcmp, ov, ksb, vsb = nsa_keys(kc, vc, ks, vs, k_g, cmp_pe, cmp_w1, cmp_b1, cmp_w2)
    pad_w = lambda a: jnp.pad(a.transpose(0, 2, 1, 3), ((0, 0), (0, 0), (WINDOW, 0), (0, 0)))
    kw_pad, vw_pad = pad_w(kw), pad_w(vw)

    def chunk(qc, pc):
        p0 = pc[0]
        kwc = lax.dynamic_slice_in_dim(kw_pad, p0, Q_CHUNK + WINDOW, axis=2)
        vwc = lax.dynamic_slice_in_dim(vw_pad, p0, Q_CHUNK + WINDOW, axis=2)
        kw_pos = p0 - WINDOW + jnp.arange(Q_CHUNK + WINDOW, dtype=jnp.int32)
        return nsa_core(qc, pc, kcmp, vcmp, ov, ksb, vsb, kwc, vwc, kw_pos)

    o = map_query_chunks(chunk, group_q(q), pos_p)
    y_p = x_p + gated_out(combine(o, gates), z, w_out)
    win_len_p = min(WINDOW, s)
    rows_p = jnp.stack([kc, vc, ks, vs], axis=2)
    win_p = jnp.stack([kw, vw], axis=2)[:, s - win_len_p:]
    db, ds, _ = x_s.shape
    p_len = kv_past.shape[1]
    wb = win_past.shape[1]
    pos_s = p_len + jnp.arange(ds, dtype=jnp.int32)
    qs, kc_s, vc_s, ks_s, vs_s, kw_s, vw_s, gates_s, zs = nsa_project(x_s, pos_s, norm_g, w_in, q_g, k_g)
    cat = lambda past, new: jnp.concatenate([past, new], axis=1)
    kcmp_s, vcmp_s, ov_s, ksb_s, vsb_s = nsa_keys(cat(kv_past[:, :, 0], kc_s), cat(kv_past[:, :, 1], vc_s),
                                                  cat(kv_past[:, :, 2], ks_s), cat(kv_past[:, :, 3], vs_s),
                                                  k_g, cmp_pe, cmp_w1, cmp_b1, cmp_w2)
    win_all = cat(win_past, jnp.stack([kw_s, vw_s], axis=2))
    kw_pos_s = p_len - wb + jnp.arange(wb + ds, dtype=jnp.int32)
    o_s = nsa_core(group_q(qs), pos_s, kcmp_s, vcmp_s, ov_s, ksb_s, vsb_s,
                   win_all[:, :, 0].transpose(0, 2, 1, 3), win_all[:, :, 1].transpose(0, 2, 1, 3), kw_pos_s)
    y_s = x_s + gated_out(combine(o_s, gates_s), zs, w_out)
    rows_s = jnp.stack([kc_s, vc_s, ks_s, vs_s], axis=2)
    return y_p, y_s, rows_p, rows_s, win_p, win_all[:, ds:]


def setup_inputs(seed: int = 0) -> dict:
    key = jax.random.key(seed)
    ks = jax.random.split(key, 24)
    nrm = lambda k, shape, sc: sc * jax.random.normal(k, shape, jnp.float32)
    n_pages = PAST_LEN // PAGE_SIZE
    n_pool = (DEC_BATCH * n_pages * 5) // 4
    win_len = min(WINDOW, PAST_LEN)
    page_table = jax.random.permutation(ks[5], n_pool)[: DEC_BATCH * n_pages].reshape(DEC_BATCH, n_pages).astype(jnp.int32)
    return {
        'x_prompt': nrm(ks[0], (BATCH, SEQ, D_MODEL), 1.0),
        'x_sample': nrm(ks[1], (DEC_BATCH, DEC_SEQ, D_MODEL), 1.0),
        'cache_moba_kv': nrm(ks[2], (N_MOBA_LAYERS, n_pool, PAGE_SIZE, 2, N_HEADS, HEAD_DIM), 1.0),
        'cache_nsa_kv': nrm(ks[3], (N_NSA_LAYERS, n_pool, PAGE_SIZE, 4, NSA_KV_GROUPS, HEAD_DIM), 1.0),
        'state_nsa_win': nrm(ks[4], (N_NSA_LAYERS, DEC_BATCH, win_len, 2, NSA_KV_GROUPS, HEAD_DIM), 1.0),
        'page_table': page_table,
        'a_norm': 1.0 + nrm(ks[6], (N_MOBA_LAYERS, D_MODEL), 0.02),
        'a_w_in': nrm(ks[7], (N_MOBA_LAYERS, D_MODEL, 4 * MIX_WIDTH), D_MODEL ** -0.5),
        'a_q_norm': 1.0 + nrm(ks[8], (N_MOBA_LAYERS, HEAD_DIM), 0.02),
        'a_k_norm': 1.0 + nrm(ks[9], (N_MOBA_LAYERS, HEAD_DIM), 0.02),
        'a_w_out': nrm(ks[10], (N_MOBA_LAYERS, MIX_WIDTH, D_MODEL), MIX_WIDTH ** -0.5),
        'b_norm': 1.0 + nrm(ks[11], (N_NSA_LAYERS, D_MODEL), 0.02),
        'b_w_in': nrm(ks[12], (N_NSA_LAYERS, D_MODEL, NSA_IN_WIDTH), D_MODEL ** -0.5),
        'b_q_norm': 1.0 + nrm(ks[13], (N_NSA_LAYERS, HEAD_DIM), 0.02),
        'b_k_norm': 1.0 + nrm(ks[14], (N_NSA_LAYERS, 3, HEAD_DIM), 0.02),
        'b_cmp_pe': nrm(ks[15], (N_NSA_LAYERS, 2, CMP_LEN, HEAD_DIM), 0.1),
        'b_cmp_w1': nrm(ks[16], (N_NSA_LAYERS, 2, CMP_LEN * HEAD_DIM, CMP_HIDDEN), (CMP_LEN * HEAD_DIM) ** -0.5),
        'b_cmp_b1': nrm(ks[17], (N_NSA_LAYERS, 2, CMP_HIDDEN), 0.02),
        'b_cmp_w2': nrm(ks[18], (N_NSA_LAYERS, 2, CMP_HIDDEN, HEAD_DIM), CMP_HIDDEN ** -0.5),
        'b_w_out': nrm(ks[19], (N_NSA_LAYERS, MIX_WIDTH, D_MODEL), MIX_WIDTH ** -0.5),
    }


def reference(x_prompt, x_sample, cache_moba_kv, cache_nsa_kv, state_nsa_win, page_table,
              a_norm, a_w_in, a_q_norm, a_k_norm, a_w_out,
              b_norm, b_w_in, b_q_norm, b_k_norm, b_cmp_pe, b_cmp_w1, b_cmp_b1, b_cmp_w2, b_w_out):
    xp, xs = x_prompt, x_sample
    moba_p, moba_s, nsa_p, nsa_s, win_p, win_s = [], [], [], [], [], []
    for layer in range(DEPTH):
        i = layer // 2
        if layer % 2 == 0:
            xp, xs, kvp, kvs = moba_layer(xp, xs, gather_pages(cache_moba_kv, i, page_table),
                                          a_norm[i], a_w_in[i], a_q_norm[i], a_k_norm[i], a_w_out[i])
            moba_p.append(kvp)
            moba_s.append(kvs)
        else:
            xp, xs, rp, rs, wp, ws = nsa_layer(xp, xs, gather_pages(cache_nsa_kv, i, page_table), state_nsa_win[i],
                                               b_norm[i], b_w_in[i], b_q_norm[i], b_k_norm[i], b_cmp_pe[i],
                                               b_cmp_w1[i], b_cmp_b1[i], b_cmp_w2[i], b_w_out[i])
            nsa_p.append(rp)
            nsa_s.append(rs)
            win_p.append(wp)
            win_s.append(ws)
    return (xp, xs, jnp.stack(moba_p), jnp.stack(moba_s), jnp.stack(nsa_p), jnp.stack(nsa_s),
            jnp.stack(win_p), jnp.stack(win_s))
```

```python
import functools

import numpy as np
import jax
import jax.numpy as jnp
from jax import lax
from jax.experimental import pallas as pl
from jax.experimental.pallas import tpu as pltpu

F32 = jnp.float32
BF16 = jnp.bfloat16

D_MODEL = 1024
N_HEADS = 16
HEAD_DIM = 64
MIX_WIDTH = N_HEADS * HEAD_DIM
ROT_DIM = HEAD_DIM // 4
ROPE_THETA = 500000.0
NORM_EPS = 1e-6
PAGE_SIZE = 128
MOBA_BLOCK = 256
MOBA_TOPK = 3
NSA_KV_GROUPS = 4
NSA_REP = N_HEADS // NSA_KV_GROUPS
KV_WIDTH = NSA_KV_GROUPS * HEAD_DIM
CMP_LEN = 32
CMP_STRIDE = 16
CMP_HIDDEN = 2 * HEAD_DIM
SEL_BLOCK = 64
SEL_TOPN = 16
WINDOW = 512
SCALE = HEAD_DIM ** -0.5

LANES = 128
NEG = -1e30
NEW_PAD = 128
VMEM_LIMIT = 48 * 1024 * 1024

PLAIN, ROPE, NORM_ROPE = 0, 1, 2
NT_DIMS = (((1,), (1,)), ((), ()))


def _cparams(sem):
    return pltpu.CompilerParams(dimension_semantics=sem, vmem_limit_bytes=VMEM_LIMIT)


def _dot(a, b):
    return jnp.dot(a, b, preferred_element_type=F32)


def _dot_nt(a, b):
    return lax.dot_general(a, b, NT_DIMS, preferred_element_type=F32)


def _split_dot(a, b):
    hi = a.astype(BF16)
    lo = (a - hi.astype(F32)).astype(BF16)
    return _dot(hi, b) + _dot(lo, b)


def _proj_kernel(x_ref, g_ref, w_ref, gain_ref, cos_ref, sa_ref, sb_ref, bd_ref, o_ref, xn_ref, *, types, tn):
    j = pl.program_id(1)

    @pl.when(j == 0)
    def _():
        x = x_ref[...]
        ms = jnp.mean(x * x, axis=-1, keepdims=True)
        xn_ref[...] = (x * lax.rsqrt(ms + NORM_EPS) * g_ref[...]).astype(BF16)

    y = _dot(xn_ref[...], w_ref[...])

    def head_norm(v):
        ss = _split_dot(v * v, bd_ref[...])
        return v * lax.rsqrt(ss * (1.0 / HEAD_DIM) + NORM_EPS) * gain_ref[...]

    def rope(v):
        rep = tn // LANES
        c = jnp.concatenate([cos_ref[...]] * rep, axis=1)
        sa = jnp.concatenate([sa_ref[...]] * rep, axis=1)
        sb = jnp.concatenate([sb_ref[...]] * rep, axis=1)
        half = ROT_DIM // 2
        return v * c + pltpu.roll(v, tn - half, 1) * sa + pltpu.roll(v, half, 1) * sb

    def emit(t):
        if t == PLAIN:
            o_ref[...] = y
        elif t == ROPE:
            o_ref[...] = rope(y)
        else:
            o_ref[...] = rope(head_norm(y))

    kinds = sorted(set(types))
    if len(kinds) == 1:
        emit(kinds[0])
    else:
        for t in kinds:
            cond = functools.reduce(jnp.logical_or, [j == jj for jj, tt in enumerate(types) if tt == t])
            pl.when(cond)(functools.partial(emit, t))


def _head_block_diag(tn):
    return jnp.asarray(np.kron(np.eye(tn // HEAD_DIM), np.ones((HEAD_DIM, HEAD_DIM))), BF16)


def _proj(x2d, g, w, types, gain_cols, tabs, tm, tn=256):
    t_rows, d = x2d.shape
    n = w.shape[1]
    assert t_rows % tm == 0 and n % tn == 0 and len(types) == n // tn
    cos, sa, sb = tabs
    assert cos.shape[0] % tm == 0
    ntab = cos.shape[0] // tm
    tab_spec = pl.BlockSpec((tm, LANES), lambda i, j: (i % ntab, 0))
    return pl.pallas_call(
        functools.partial(_proj_kernel, types=tuple(types), tn=tn),
        grid=(t_rows // tm, n // tn),
        in_specs=[
            pl.BlockSpec((tm, d), lambda i, j: (i, 0)),
            pl.BlockSpec((1, d), lambda i, j: (0, 0)),
            pl.BlockSpec((d, tn), lambda i, j: (0, j)),
            pl.BlockSpec((1, tn), lambda i, j: (0, j)),
            tab_spec, tab_spec, tab_spec,
            pl.BlockSpec((tn, tn), lambda i, j: (0, 0)),
        ],
        out_specs=pl.BlockSpec((tm, tn), lambda i, j: (i, j)),
        out_shape=jax.ShapeDtypeStruct((t_rows, n), F32),
        scratch_shapes=[pltpu.VMEM((tm, d), BF16)],
        compiler_params=_cparams(("parallel", "arbitrary")),
        name="rmsnorm_in_proj",
    )(x2d, g.reshape(1, d), w.astype(BF16), gain_cols.reshape(1, n).astype(F32), cos, sa, sb,
      _head_block_diag(tn))


def _rope_tables(pos):
    half = ROT_DIM // 2
    inv_freq = jnp.float32(ROPE_THETA) ** (-jnp.arange(half, dtype=F32) / half)
    ang = pos.astype(F32)[:, None] * inv_freq[None, :]
    cos, sin = jnp.cos(ang), jnp.sin(ang)
    t = pos.shape[0]
    rest = HEAD_DIM - ROT_DIM
    z8, zr, one_r = jnp.zeros((t, half), F32), jnp.zeros((t, rest), F32), jnp.ones((t, rest), F32)
    c = jnp.concatenate([cos, cos, one_r], axis=1)
    sa = jnp.concatenate([-sin, z8, zr], axis=1)
    sb = jnp.concatenate([z8, sin, zr], axis=1)
    rep = LANES // HEAD_DIM
    return tuple(jnp.concatenate([a] * rep, axis=1) for a in (c, sa, sb))


def _out_proj_kernel(o_ref, z_ref, x_ref, w_ref, y_ref):
    z = z_ref[...]
    gated = o_ref[...] * (z * (1.0 / (1.0 + jnp.exp(-z))))
    y_ref[...] = x_ref[...] + _dot(gated.astype(BF16), w_ref[...])


def _out_proj(o2d, z2d, x2d, w_out, tm):
    t_rows, d = x2d.shape
    k = o2d.shape[1]
    row = lambda i: (i, 0)
    return pl.pallas_call(
        _out_proj_kernel,
        grid=(t_rows // tm,),
        in_specs=[pl.BlockSpec((tm, k), row), pl.BlockSpec((tm, k), row), pl.BlockSpec((tm, d), row),
                  pl.BlockSpec((k, d), lambda i: (0, 0))],
        out_specs=pl.BlockSpec((tm, d), row),
        out_shape=jax.ShapeDtypeStruct((t_rows, d), F32),
        compiler_params=_cparams(("parallel",)),
        name="gated_out_proj",
    )(o2d, z2d, x2d, w_out.astype(BF16))


def _top_k_mask(scores, lane, k, width):
    sel = jnp.zeros(scores.shape, F32)
    lane = lane.astype(F32)
    for _ in range(k):
        mx = jnp.max(scores, axis=-1, keepdims=True)
        first = jnp.min(jnp.where(scores == mx, lane, float(width)), axis=-1, keepdims=True)
        hit = lane == first
        sel = jnp.where(hit & (mx > -jnp.inf), 1.0, sel)
        scores = jnp.where(hit, -jnp.inf, scores)
    return sel


def _online_update(s, mask, v, m_ref, l_ref, acc_ref):
    s = jnp.where(mask, s, NEG)
    m_old = m_ref[...]
    m_new = jnp.maximum(m_old, jnp.max(s, axis=-1, keepdims=True))
    alpha = jnp.exp(m_old - m_new)
    p = jnp.where(mask, jnp.exp(s - m_new), 0.0)
    l_ref[...] = alpha * l_ref[...] + jnp.sum(p, axis=-1, keepdims=True)
    acc_ref[...] = alpha * acc_ref[...] + _dot(p.astype(BF16), v)
    m_ref[...] = m_new


def _reset(m_ref, l_ref, acc_ref):
    m_ref[...] = jnp.full(m_ref.shape, NEG, F32)
    l_ref[...] = jnp.zeros(l_ref.shape, F32)
    acc_ref[...] = jnp.zeros(acc_ref.shape, F32)


def _block_mean_kernel(k_ref, o_ref):
    o_ref[0, 0] = jnp.sum(k_ref[0], axis=0, keepdims=True) * (1.0 / MOBA_BLOCK)


def _block_mean(kv3):
    b, s, _ = kv3.shape
    nb = s // MOBA_BLOCK
    return pl.pallas_call(
        _block_mean_kernel,
        grid=(b, nb),
        in_specs=[pl.BlockSpec((1, MOBA_BLOCK, MIX_WIDTH), lambda i, j: (i, j, 0))],
        out_specs=pl.BlockSpec((1, 1, 1, MIX_WIDTH), lambda i, j: (i, j, 0, 0)),
        out_shape=jax.ShapeDtypeStruct((b, nb, 1, MIX_WIDTH), F32),
        compiler_params=_cparams(("parallel", "parallel")),
        name="moba_block_mean",
    )(kv3)


def _moba_attn_kernel(q_ref, kmt_ref, kt_ref, v_ref, o_ref, sel_ref, m_ref, l_ref, acc_ref):
    i = pl.program_id(2)
    blk = MOBA_BLOCK
    q = q_ref[0, 0]
    nb = kmt_ref.shape[-1]
    lane = lax.broadcasted_iota(jnp.int32, (blk, nb), 1)
    sb = jnp.where(lane < i, _dot(q, kmt_ref[0, 0]), -jnp.inf)
    sel_ref[...] = _top_k_mask(sb, lane, min(MOBA_TOPK, nb), nb)
    _reset(m_ref, l_ref, acc_ref)

    def past_block(n, carry):
        picked = jnp.sum(jnp.where(lane == n, sel_ref[...], 0.0), axis=-1, keepdims=True) > 0.5
        _online_update(_dot(q, kt_ref[0, 0, n]), picked, v_ref[0, 0, n], m_ref, l_ref, acc_ref)
        return carry

    lax.fori_loop(0, i, past_block, 0)
    row = lax.broadcasted_iota(jnp.int32, (blk, blk), 0)
    col = lax.broadcasted_iota(jnp.int32, (blk, blk), 1)
    _online_update(_dot(q, kt_ref[0, 0, i]), col <= row, v_ref[0, 0, i], m_ref, l_ref, acc_ref)
    o_ref[0, 0] = acc_ref[...] / l_ref[...]


def _moba_attn(qh, kmt, kt, v):
    b, h, s, d = qh.shape
    nb = s // MOBA_BLOCK
    blk = MOBA_BLOCK
    return pl.pallas_call(
        _moba_attn_kernel,
        grid=(b, h, nb),
        in_specs=[
            pl.BlockSpec((1, 1, blk, d), lambda bi, hi, i: (bi, hi, i, 0)),
            pl.BlockSpec((1, 1, d, nb), lambda bi, hi, i: (bi, hi, 0, 0)),
            pl.BlockSpec((1, 1, nb, d, blk), lambda bi, hi, i: (bi, hi, 0, 0, 0)),
            pl.BlockSpec((1, 1, nb, blk, d), lambda bi, hi, i: (bi, hi, 0, 0, 0)),
        ],
        out_specs=pl.BlockSpec((1, 1, blk, d), lambda bi, hi, i: (bi, hi, i, 0)),
        out_shape=jax.ShapeDtypeStruct((b, h, s, d), F32),
        scratch_shapes=[pltpu.VMEM((blk, nb), F32), pltpu.VMEM((blk, 1), F32), pltpu.VMEM((blk, 1), F32),
                        pltpu.VMEM((blk, d), F32)],
        compiler_params=_cparams(("parallel", "parallel", "arbitrary")),
        name="moba_prompt_attn",
    )(qh, kmt, kt, v)


def _moba_sample_kernel(pt_ref, k0_ref, k1_ref, v0_ref, v1_ref, qbd_ref, kn_ref, vn_ref, o_ref,
                        ob_ref, mb_ref, lb_ref, sc_ref, w_ref, *, n_new):
    del pt_ref
    j = pl.program_id(1)
    nbp = pl.num_programs(1)
    qbd = qbd_ref[0]
    rows = qbd.shape[0]
    s = jnp.concatenate([_dot_nt(qbd, k0_ref[0].astype(BF16)), _dot_nt(qbd, k1_ref[0].astype(BF16))], axis=1)
    sc_ref[j] = jnp.sum(s, axis=-1, keepdims=True) * (1.0 / MOBA_BLOCK)
    m = jnp.max(s, axis=-1, keepdims=True)
    p = jnp.exp(s - m)
    mb_ref[j] = m
    lb_ref[j] = jnp.sum(p, axis=-1, keepdims=True)
    pb = p.astype(BF16)
    ob_ref[j] = _dot(pb[:, :PAGE_SIZE], v0_ref[0].astype(BF16)) + _dot(pb[:, PAGE_SIZE:], v1_ref[0].astype(BF16))

    @pl.when(j == nbp - 1)
    def _():
        nb = sc_ref.shape[0]
        blk_id = lax.broadcasted_iota(jnp.int32, sc_ref.shape, 0).astype(F32)
        sc = sc_ref[...]
        sel = jnp.zeros(sc.shape, F32)
        for _ in range(MOBA_TOPK):
            mx = jnp.max(sc, axis=0, keepdims=True)
            first = jnp.min(jnp.where(sc == mx, blk_id, float(nb)), axis=0, keepdims=True)
            hit = blk_id == first
            sel = jnp.where(hit, 1.0, sel)
            sc = jnp.where(hit, -jnp.inf, sc)
        s_new = _dot_nt(qbd, kn_ref[0])
        t_row = lax.broadcasted_iota(jnp.int32, s_new.shape, 0) // N_HEADS
        t_col = lax.broadcasted_iota(jnp.int32, s_new.shape, 1)
        ok = (t_col <= t_row) & (t_col < n_new)
        m_new = jnp.max(jnp.where(ok, s_new, NEG), axis=-1, keepdims=True)
        p_new = jnp.where(ok, jnp.exp(s_new - m_new), 0.0)
        l_new = jnp.sum(p_new, axis=-1, keepdims=True)
        o_new = _dot(p_new.astype(BF16), vn_ref[0])
        picked = sel > 0.5
        m_all = jnp.maximum(jnp.max(jnp.where(picked, mb_ref[...], NEG), axis=0), m_new)
        w = jnp.where(picked, jnp.exp(mb_ref[...] - m_all[None]), 0.0)
        w_ref[...] = w
        w_new = jnp.exp(m_new - m_all)
        den = jnp.sum(w * lb_ref[...], axis=0) + w_new * l_new

        def add_block(n, acc):
            return acc + w_ref[n] * ob_ref[n]

        num = lax.fori_loop(0, nb, add_block, w_new * o_new)
        o_ref[0] = num / den


def _moba_sample(page_table, cache_view, qbd, k_new, v_new, n_new):
    db, rows, _ = qbd.shape
    n_pages = page_table.shape[1]
    ppb = MOBA_BLOCK // PAGE_SIZE
    assert ppb == 2 and n_pages % ppb == 0
    nbp = n_pages // ppb
    assert nbp >= MOBA_TOPK
    pt = page_table.reshape(-1)

    def page_spec(u, half):
        return pl.BlockSpec((1, PAGE_SIZE, MIX_WIDTH), lambda b, j, pt_ref: (pt_ref[b * n_pages + ppb * j + u], 0, half))

    per_b = lambda b, j, pt_ref: (b, 0, 0)
    grid_spec = pltpu.PrefetchScalarGridSpec(
        num_scalar_prefetch=1,
        grid=(db, nbp),
        in_specs=[page_spec(0, 0), page_spec(1, 0), page_spec(0, 1), page_spec(1, 1),
                  pl.BlockSpec((1, rows, MIX_WIDTH), per_b),
                  pl.BlockSpec((1, NEW_PAD, MIX_WIDTH), per_b),
                  pl.BlockSpec((1, NEW_PAD, MIX_WIDTH), per_b)],
        out_specs=pl.BlockSpec((1, rows, MIX_WIDTH), per_b),
        scratch_shapes=[pltpu.VMEM((nbp, rows, MIX_WIDTH), F32), pltpu.VMEM((nbp, rows, 1), F32),
                        pltpu.VMEM((nbp, rows, 1), F32), pltpu.VMEM((nbp, rows, 1), F32),
                        pltpu.VMEM((nbp, rows, 1), F32)],
    )
    return pl.pallas_call(
        functools.partial(_moba_sample_kernel, n_new=n_new),
        grid_spec=grid_spec,
        out_shape=jax.ShapeDtypeStruct((db, rows, MIX_WIDTH), F32),
        compiler_params=_cparams(("parallel", "arbitrary")),
        name="moba_sample_attn",
    )(pt, cache_view, cache_view, cache_view, cache_view, qbd, k_new, v_new)


def _compress_kernel(x_ref, pe_ref, w1_ref, b1_ref, w2_ref, kg_ref, o_ref):
    c = pl.program_id(0)
    x = x_ref[0, 0]
    nr = x.shape[0]
    half = x.shape[1]
    pe = pe_ref[0]
    p0 = _dot((x + pe[0:1]).astype(BF16), w1_ref[0, :half, :])
    p1 = _dot((x + pe[1:2]).astype(BF16), w1_ref[0, half:, :])
    h = p0 + pltpu.roll(p1, nr - 1, 0) + b1_ref[0]
    g = 0.5 * h * (1.0 + jnp.tanh(0.7978845608028654 * (h + 0.044715 * (h * h * h))))
    y = _dot(g.astype(BF16), w2_ref[0])
    yn = y * lax.rsqrt(jnp.mean(y * y, axis=-1, keepdims=True) + NORM_EPS) * kg_ref[...]
    y = jnp.where(c == 0, yn, y)
    row = lax.broadcasted_iota(jnp.int32, y.shape, 0)
    o_ref[0, 0] = jnp.where(row < nr - 1, y, 0.0)


def _compress(x16, pe, w1, b1, w2, kg0):
    _, bb, nr, width = x16.shape
    per_c = lambda c, i: (c, 0, 0)
    return pl.pallas_call(
        _compress_kernel,
        grid=(2, bb),
        in_specs=[
            pl.BlockSpec((1, 1, nr, width), lambda c, i: (c, i, 0, 0)),
            pl.BlockSpec((1, 2, width), per_c),
            pl.BlockSpec((1, CMP_LEN * HEAD_DIM, CMP_HIDDEN), per_c),
            pl.BlockSpec((1, 1, CMP_HIDDEN), per_c),
            pl.BlockSpec((1, CMP_HIDDEN, HEAD_DIM), per_c),
            pl.BlockSpec((1, HEAD_DIM), lambda c, i: (0, 0)),
        ],
        out_specs=pl.BlockSpec((1, 1, nr, HEAD_DIM), lambda c, i: (c, i, 0, 0)),
        out_shape=jax.ShapeDtypeStruct((2, bb, nr, HEAD_DIM), F32),
        compiler_params=_cparams(("parallel", "parallel")),
        name="nsa_compress",
    )(x16, pe.reshape(2, 2, width), w1.astype(BF16), b1.reshape(2, 1, CMP_HIDDEN), w2.astype(BF16),
      kg0.reshape(1, HEAD_DIM))


def _page_gather_kernel(pt_ref, *refs, n_in):
    del pt_ref
    o_ref = refs[n_in]
    for u in range(n_in):
        o_ref[0, u * PAGE_SIZE:(u + 1) * PAGE_SIZE, :] = refs[u][0]


def _page_gather(page_table, cache_view, lane_block, width, pages_per_step=8):
    db, n_pages = page_table.shape
    assert n_pages % pages_per_step == 0
    pt = page_table.reshape(-1)

    def page_spec(u):
        return pl.BlockSpec((1, PAGE_SIZE, width),
                            lambda b, p, pt_ref: (pt_ref[b * n_pages + p * pages_per_step + u], 0, lane_block))

    grid_spec = pltpu.PrefetchScalarGridSpec(
        num_scalar_prefetch=1,
        grid=(db, n_pages // pages_per_step),
        in_specs=[page_spec(u) for u in range(pages_per_step)],
        out_specs=pl.BlockSpec((1, pages_per_step * PAGE_SIZE, width), lambda b, p, pt_ref: (b, p, 0)),
    )
    return pl.pallas_call(
        functools.partial(_page_gather_kernel, n_in=pages_per_step),
        grid_spec=grid_spec,
        out_shape=jax.ShapeDtypeStruct((db, n_pages * PAGE_SIZE, width), cache_view.dtype),
        compiler_params=_cparams(("parallel", "parallel")),
        name="nsa_page_gather",
    )(pt, *([cache_view] * pages_per_step))


def _overlap(n_rows, n_cmp, n_cols, n_sel):
    i = np.arange(n_rows)[:, None]
    j = np.arange(n_cols)[None, :]
    st = i * CMP_STRIDE
    j0 = j * SEL_BLOCK
    ov = (st < j0 + SEL_BLOCK) & (st + CMP_LEN > j0) & (i < n_cmp) & (j < n_sel)
    return jnp.asarray(ov, BF16)


def _sigmoid(x):
    return 1.0 / (1.0 + jnp.exp(-x))


def _nsa_select(p_sum, ov, pos, n_cols):
    imp = _split_dot(p_sum, ov)
    j = lax.broadcasted_iota(jnp.int32, imp.shape, 1)
    cur = pos // SEL_BLOCK
    forced = (j == 0) | (j == cur) | (j == cur - 1)
    imp = jnp.where(forced, jnp.inf, jnp.where(j <= cur, imp, -jnp.inf))
    return _top_k_mask(imp, j, min(SEL_TOPN, n_cols), n_cols)


def _masked_softmax(s, mask):
    s = jnp.where(mask, s, NEG)
    m = jnp.max(s, axis=-1, keepdims=True)
    p = jnp.where(mask, jnp.exp(s - m), 0.0)
    return p / jnp.maximum(jnp.sum(p, axis=-1, keepdims=True), 1e-30)


def _nsa_attn_kernel(q_ref, kct_ref, vc_ref, kst_ref, vs_ref, kwt_ref, vw_ref, gt_ref, ov_ref, o_ref,
                     sel_ref, m_ref, l_ref, acc_ref, *, tq, kt_size, n_cmp):
    i = pl.program_id(2)
    r = NSA_REP
    q = q_ref[0, 0].reshape(r * tq, HEAD_DIM)
    q0 = i * tq
    pos = q0 + lax.broadcasted_iota(jnp.int32, (tq, 1), 0)
    pos4 = q0 + lax.broadcasted_iota(jnp.int32, (r * tq, 1), 0) % tq

    ncp = kct_ref.shape[-1]
    n_id = lax.broadcasted_iota(jnp.int32, (r * tq, ncp), 1)
    cmask = (n_id * CMP_STRIDE + (CMP_LEN - 1) <= pos4) & (n_id < n_cmp)
    pc = _masked_softmax(_dot(q, kct_ref[0, 0]), cmask)
    o_c = _dot(pc.astype(BF16), vc_ref[0, 0])
    p_sum = pc[0:tq]
    for rr in range(1, r):
        p_sum = p_sum + pc[rr * tq:(rr + 1) * tq]
    nsp = ov_ref.shape[-1]
    sel_ref[...] = _nsa_select(p_sum, ov_ref[...], pos, nsp).astype(BF16)

    _reset(m_ref, l_ref, acc_ref)
    blocks_per_tile = kt_size // SEL_BLOCK

    def sel_tile(kt, carry):
        jrow = lax.broadcasted_iota(jnp.int32, (nsp, kt_size), 0)
        kcol = lax.broadcasted_iota(jnp.int32, (nsp, kt_size), 1)
        expand = jnp.where(jrow == kt * blocks_per_tile + kcol // SEL_BLOCK, 1.0, 0.0).astype(BF16)
        chosen = _dot(sel_ref[...], expand)
        kpos = kt * kt_size + lax.broadcasted_iota(jnp.int32, (tq, kt_size), 1)
        keep = jnp.where(kpos <= pos, chosen, 0.0)
        mask4 = jnp.concatenate([keep] * r, axis=0) > 0.5
        _online_update(_dot(q, kst_ref[0, 0, kt]), mask4, vs_ref[0, 0, kt], m_ref, l_ref, acc_ref)
        return carry

    lax.fori_loop(0, (q0 + tq - 1) // kt_size + 1, sel_tile, 0)
    o_s = acc_ref[...] / l_ref[...]

    _reset(m_ref, l_ref, acc_ref)

    def win_tile(w, carry):
        kpos = w * tq + lax.broadcasted_iota(jnp.int32, (r * tq, tq), 1)
        mask4 = (kpos <= pos4) & (kpos > pos4 - WINDOW)
        _online_update(_dot(q, kwt_ref[0, 0, w]), mask4, vw_ref[0, 0, w], m_ref, l_ref, acc_ref)
        return carry

    lax.fori_loop(jnp.maximum(i - WINDOW // tq, 0), i + 1, win_tile, 0)
    o_w = acc_ref[...] / l_ref[...]

    gates = _sigmoid(gt_ref[0, 0])
    for rr in range(r):
        rows = slice(rr * tq, (rr + 1) * tq)
        o_ref[0, 0, rr] = (gates[:, 3 * rr:3 * rr + 1] * o_c[rows] + gates[:, 3 * rr + 1:3 * rr + 2] * o_s[rows]
                           + gates[:, 3 * rr + 2:3 * rr + 3] * o_w[rows])


def _nsa_attn(qg, kct, vc, kst, vs, kwt, vw, gt, ov, n_cmp, tq=128):
    b, g, r, s, d = qg.shape
    ncp = kct.shape[-1]
    nkt, kt_size = kst.shape[2], kst.shape[-1]
    nwt = kwt.shape[2]
    nsp = ov.shape[-1]
    assert kwt.shape[-1] == tq and WINDOW % tq == 0 and s % tq == 0
    per_bg = lambda n: (lambda bi, gi, i: (bi, gi) + (0,) * n)
    return pl.pallas_call(
        functools.partial(_nsa_attn_kernel, tq=tq, kt_size=kt_size, n_cmp=n_cmp),
        grid=(b, g, s // tq),
        in_specs=[
            pl.BlockSpec((1, 1, r, tq, d), lambda bi, gi, i: (bi, gi, 0, i, 0)),
            pl.BlockSpec((1, 1, d, ncp), per_bg(2)),
            pl.BlockSpec((1, 1, ncp, d), per_bg(2)),
            pl.BlockSpec((1, 1, nkt, d, kt_size), per_bg(3)),
            pl.BlockSpec((1, 1, nkt, kt_size, d), per_bg(3)),
            pl.BlockSpec((1, 1, nwt, d, tq), per_bg(3)),
            pl.BlockSpec((1, 1, nwt, tq, d), per_bg(3)),
            pl.BlockSpec((1, 1, tq, LANES), lambda bi, gi, i: (bi, gi, i, 0)),
            pl.BlockSpec((ncp, nsp), lambda bi, gi, i: (0, 0)),
        ],
        out_specs=pl.BlockSpec((1, 1, r, tq, d), lambda bi, gi, i: (bi, gi, 0, i, 0)),
        out_shape=jax.ShapeDtypeStruct((b, g, r, s, d), F32),
        scratch_shapes=[pltpu.VMEM((tq, nsp), BF16), pltpu.VMEM((r * tq, 1), F32), pltpu.VMEM((r * tq, 1), F32),
                        pltpu.VMEM((r * tq, d), F32)],
        compiler_params=_cparams(("parallel", "parallel", "arbitrary")),
        name="nsa_prompt_attn",
    )(qg, kct, vc, kst, vs, kwt, vw, gt, ov)


def _nsa_sample_kernel(pt_ref, pg_ref, qbd_ref, kc_ref, vc_ref, win_ref, ksn_ref, vsn_ref, kwn_ref, vwn_ref,
                       gt_ref, ov_ref, o_ref, sel_ref, oc_ref, m_ref, l_ref, acc_ref, *, n_new, p_len, n_cmp):
    del pt_ref
    p = pl.program_id(1)
    n_pg = pl.num_programs(1)
    qbd = qbd_ref[0]
    rows = qbd.shape[0]
    tg = rows // NSA_REP
    t_of_row = (lax.broadcasted_iota(jnp.int32, (rows, 1), 0) % tg) // NSA_KV_GROUPS
    pos = p_len + t_of_row

    @pl.when(p == 0)
    def _():
        ncp = kc_ref.shape[1]
        n_id = lax.broadcasted_iota(jnp.int32, (rows, ncp), 1)
        cmask = (n_id * CMP_STRIDE + (CMP_LEN - 1) <= pos) & (n_id < n_cmp)
        pc = _masked_softmax(_dot_nt(qbd, kc_ref[0]), cmask)
        oc_ref[...] = _dot(pc.astype(BF16), vc_ref[0])
        p_sum = pc[0:tg]
        for rr in range(1, NSA_REP):
            p_sum = p_sum + pc[rr * tg:(rr + 1) * tg]
        sel = _nsa_select(p_sum, ov_ref[...], pos[0:tg], ov_ref.shape[-1])
        sel_ref[...] = jnp.concatenate([sel] * NSA_REP, axis=0)
        _reset(m_ref, l_ref, acc_ref)

    page = pg_ref[0]
    blocks_per_page = PAGE_SIZE // SEL_BLOCK
    lane = lax.broadcasted_iota(jnp.int32, sel_ref.shape, 1)
    key = lax.broadcasted_iota(jnp.int32, (rows, PAGE_SIZE), 1)
    chosen = jnp.zeros((rows, PAGE_SIZE), F32)
    for u in range(blocks_per_page):
        col = jnp.sum(jnp.where(lane == p * blocks_per_page + u, sel_ref[...], 0.0), axis=-1, keepdims=True)
        chosen = jnp.where(key // SEL_BLOCK == u, col, chosen)
    _online_update(_dot_nt(qbd, page[:, :KV_WIDTH].astype(BF16)), chosen > 0.5, page[:, KV_WIDTH:].astype(BF16),
                   m_ref, l_ref, acc_ref)

    @pl.when(p == n_pg - 1)
    def _():
        t_col = lax.broadcasted_iota(jnp.int32, (rows, NEW_PAD), 1)
        new_ok = (t_col <= t_of_row) & (t_col < n_new)
        _online_update(_dot_nt(qbd, ksn_ref[0]), new_ok, vsn_ref[0], m_ref, l_ref, acc_ref)
        o_s = acc_ref[...] / l_ref[...]
        _reset(m_ref, l_ref, acc_ref)
        win = win_ref[0]
        wb = win.shape[0]
        kw_pos = p_len - wb + lax.broadcasted_iota(jnp.int32, (rows, wb), 1)
        w_ok = (kw_pos <= pos) & (kw_pos > pos - WINDOW) & (kw_pos >= 0)
        _online_update(_dot_nt(qbd, win[:, :KV_WIDTH].astype(BF16)), w_ok, win[:, KV_WIDTH:].astype(BF16),
                       m_ref, l_ref, acc_ref)
        _online_update(_dot_nt(qbd, kwn_ref[0]), new_ok, vwn_ref[0], m_ref, l_ref, acc_ref)
        o_w = acc_ref[...] / l_ref[...]
        gates = _sigmoid(gt_ref[0])
        o_ref[0] = gates[:, 0:1] * oc_ref[...] + gates[:, 1:2] * o_s + gates[:, 2:3] * o_w


def _nsa_sample(page_table, cache_view, qbd, kc, vc, win, ks_new, vs_new, kw_new, vw_new, gt, ov, n_new, p_len, n_cmp):
    db, rows, _ = qbd.shape
    n_pages = page_table.shape[1]
    ncp = kc.shape[1]
    wb = win.shape[1]
    nsp = ov.shape[-1]
    pt = page_table.reshape(-1)
    per_b = lambda b, p, pt_ref: (b, 0, 0)
    new_spec = pl.BlockSpec((1, NEW_PAD, KV_WIDTH), per_b)
    grid_spec = pltpu.PrefetchScalarGridSpec(
        num_scalar_prefetch=1,
        grid=(db, n_pages),
        in_specs=[
            pl.BlockSpec((1, PAGE_SIZE, 2 * KV_WIDTH), lambda b, p, pt_ref: (pt_ref[b * n_pages + p], 0, 1)),
            pl.BlockSpec((1, rows, KV_WIDTH), per_b),
            pl.BlockSpec((1, ncp, KV_WIDTH), per_b),
            pl.BlockSpec((1, ncp, KV_WIDTH), per_b),
            pl.BlockSpec((1, wb, 2 * KV_WIDTH), per_b),
            new_spec, new_spec, new_spec, new_spec,
            pl.BlockSpec((1, rows, LANES), per_b),
            pl.BlockSpec((ncp, nsp), lambda b, p, pt_ref: (0, 0)),
        ],
        out_specs=pl.BlockSpec((1, rows, KV_WIDTH), per_b),
        scratch_shapes=[pltpu.VMEM((rows, nsp), F32), pltpu.VMEM((rows, KV_WIDTH), F32),
                        pltpu.VMEM((rows, 1), F32), pltpu.VMEM((rows, 1), F32), pltpu.VMEM((rows, KV_WIDTH), F32)],
    )
    return pl.pallas_call(
        functools.partial(_nsa_sample_kernel, n_new=n_new, p_len=p_len, n_cmp=n_cmp),
        grid_spec=grid_spec,
        out_shape=jax.ShapeDtypeStruct((db, rows, KV_WIDTH), F32),
        compiler_params=_cparams(("parallel", "arbitrary")),
        name="nsa_sample_attn",
    )(pt, cache_view, qbd, kc, vc, win, ks_new, vs_new, kw_new, vw_new, gt, ov)


def _row_tile(t_rows):
    return 512 if t_rows % 512 == 0 else t_rows


def _pad_rows(a, n):
    return jnp.pad(a, ((0, 0), (0, n - a.shape[1]), (0, 0)))


def _moba_layer(x_p, x_s, cache, page_table, norm_g, w_in, q_g, k_g, w_out):
    b, s, d = x_p.shape
    db, ds, _ = x_s.shape
    h, hd = N_HEADS, HEAD_DIM
    n_pages = page_table.shape[1]
    p_len = n_pages * PAGE_SIZE
    assert s % MOBA_BLOCK == 0 and p_len % MOBA_BLOCK == 0 and ds <= MOBA_BLOCK
    wq, wkv, wz = w_in[:, :MIX_WIDTH], w_in[:, MIX_WIDTH:3 * MIX_WIDTH], w_in[:, 3 * MIX_WIDTH:]
    gq = jnp.tile(q_g, h)
    gkv = jnp.concatenate([jnp.tile(k_g, h), jnp.ones((MIX_WIDTH,), F32)])
    ones = jnp.ones((MIX_WIDTH,), F32)

    def project(x2d, pos):
        tabs = _rope_tables(pos)
        tm = _row_tile(x2d.shape[0])
        tabs = tabs if pos.shape[0] % tm == 0 else tuple(jnp.tile(t, (tm // pos.shape[0], 1)) for t in tabs)
        q = _proj(x2d, norm_g, wq, [NORM_ROPE] * 4, gq, tabs, tm)
        kv = _proj(x2d, norm_g, wkv, [NORM_ROPE] * 4 + [PLAIN] * 4, gkv, tabs, tm)
        z = _proj(x2d, norm_g, wz, [PLAIN] * 4, ones, tabs, tm)
        return q, kv, z

    xp2 = x_p.reshape(b * s, d)
    q, kv, z = project(xp2, jnp.arange(s, dtype=jnp.int32))
    nb = s // MOBA_BLOCK
    kv3 = kv.reshape(b, s, 2 * MIX_WIDTH)
    kmt = _block_mean(kv3).reshape(b, nb, h, hd).transpose(0, 2, 3, 1).astype(BF16)
    qh = (q * SCALE).astype(BF16).reshape(b, s, h, hd).transpose(0, 2, 1, 3)
    kv5 = kv.astype(BF16).reshape(b, nb, MOBA_BLOCK, 2, h, hd)
    kt = kv5[:, :, :, 0].transpose(0, 3, 1, 4, 2)
    vv = kv5[:, :, :, 1].transpose(0, 3, 1, 2, 4)
    o = _moba_attn(qh, kmt, kt, vv).transpose(0, 2, 1, 3).reshape(b * s, MIX_WIDTH)
    y_p = _out_proj(o, z, xp2, w_out, _row_tile(b * s)).reshape(b, s, d)

    xs2 = x_s.reshape(db * ds, d)
    qs, kvs, zs = project(xs2, jnp.tile(p_len + jnp.arange(ds, dtype=jnp.int32), db))
    rows = ds * h
    q_rows = (qs * SCALE).reshape(db, rows, 1, hd)
    head_of_row = jnp.arange(rows) % h
    diag = (head_of_row[:, None] == jnp.arange(h)[None, :]).astype(F32)
    qbd = (q_rows * diag[None, :, :, None]).astype(BF16).reshape(db, rows, MIX_WIDTH)
    kvs3 = kvs.astype(BF16).reshape(db, ds, 2 * MIX_WIDTH)
    k_new = _pad_rows(kvs3[:, :, :MIX_WIDTH], NEW_PAD)
    v_new = _pad_rows(kvs3[:, :, MIX_WIDTH:], NEW_PAD)
    cache_view = cache.reshape(cache.shape[0], PAGE_SIZE, 2 * MIX_WIDTH)
    o_full = _moba_sample(page_table, cache_view, qbd, k_new, v_new, ds)
    o6 = o_full.reshape(db, ds, h, h, hd)
    o_s = o6[:, :, jnp.arange(h), jnp.arange(h), :].reshape(db * ds, MIX_WIDTH)
    y_s = _out_proj(o_s, zs, xs2, w_out, db * ds).reshape(db, ds, d)
    return (y_p, y_s, kv.reshape(b, s, 2, h, hd), kvs.reshape(db, ds, 2, h, hd))


def _nsa_layer(x_p, x_s, cache, win_past, page_table, norm_g, w_in, q_g, k_g, cmp_pe, cmp_w1, cmp_b1, cmp_w2, w_out):
    b, s, d = x_p.shape
    db, ds, _ = x_s.shape
    h, hd, g, r = N_HEADS, HEAD_DIM, NSA_KV_GROUPS, NSA_REP
    n_pages = page_table.shape[1]
    p_len = n_pages * PAGE_SIZE
    wb = win_past.shape[1]
    kt_size = 256
    tq = 128
    assert s % kt_size == 0 and s % CMP_STRIDE == 0 and ds <= SEL_BLOCK and p_len % SEL_BLOCK == 0

    o_q, o_rows, o_win = MIX_WIDTH, MIX_WIDTH + 4 * KV_WIDTH, MIX_WIDTH + 6 * KV_WIDTH
    o_z = o_win + 3 * h
    wq, wrows, wwin = w_in[:, :o_q], w_in[:, o_q:o_rows], w_in[:, o_rows:o_win]
    wgt = jnp.pad(w_in[:, o_win:o_z], ((0, 0), (0, LANES - 3 * h)))
    wz = w_in[:, o_z:]
    ones_kv = jnp.ones((KV_WIDTH,), F32)
    gq = jnp.tile(q_g, h)
    grows = jnp.concatenate([ones_kv, ones_kv, jnp.tile(k_g[1], g), ones_kv])
    gwin = jnp.concatenate([jnp.tile(k_g[2], g), ones_kv])

    def project(x2d, pos):
        tabs = _rope_tables(pos)
        tm = _row_tile(x2d.shape[0])
        tabs = tabs if pos.shape[0] % tm == 0 else tuple(jnp.tile(t, (tm // pos.shape[0], 1)) for t in tabs)
        q = _proj(x2d, norm_g, wq, [NORM_ROPE] * 4, gq, tabs, tm)
        rows_ = _proj(x2d, norm_g, wrows, [ROPE, PLAIN, NORM_ROPE, PLAIN], grows, tabs, tm)
        win_ = _proj(x2d, norm_g, wwin, [NORM_ROPE, PLAIN], gwin, tabs, tm)
        gt_ = _proj(x2d, norm_g, wgt, [PLAIN], jnp.ones((LANES,), F32), tabs, tm, tn=LANES)
        z = _proj(x2d, norm_g, wz, [PLAIN] * 4, jnp.ones((MIX_WIDTH,), F32), tabs, tm)
        return q, rows_, win_, gt_, z

    pe_flat = cmp_pe.reshape(2, 2, CMP_STRIDE * hd)

    xp2 = x_p.reshape(b * s, d)
    q, rows_p, win_p, gt, z = project(xp2, jnp.arange(s, dtype=jnp.int32))
    nr = s // CMP_STRIDE
    n_cmp = (s - CMP_LEN) // CMP_STRIDE + 1
    n_sel = -(-s // SEL_BLOCK)
    nsp = -(-n_sel // LANES) * LANES
    rows6 = rows_p.reshape(b, nr, CMP_STRIDE, 4, g, hd)
    x16 = rows6[:, :, :, :2].transpose(3, 0, 4, 1, 2, 5).reshape(2, b * g, nr, CMP_STRIDE * hd)
    cmp_kv = _compress(x16, pe_flat, cmp_w1, cmp_b1, cmp_w2, k_g[0]).astype(BF16).reshape(2, b, g, nr, hd)
    kct = cmp_kv[0].transpose(0, 1, 3, 2)
    vc = cmp_kv[1]
    rows_bf = rows_p.astype(BF16).reshape(b, s // kt_size, kt_size, 4, g, hd)
    kst = rows_bf[:, :, :, 2].transpose(0, 3, 1, 4, 2)
    vs = rows_bf[:, :, :, 3].transpose(0, 3, 1, 2, 4)
    win_bf = win_p.astype(BF16).reshape(b, s // tq, tq, 2, g, hd)
    kwt = win_bf[:, :, :, 0].transpose(0, 3, 1, 4, 2)
    vw = win_bf[:, :, :, 1].transpose(0, 3, 1, 2, 4)
    qg = (q * SCALE).astype(BF16).reshape(b, s, g, r, hd).transpose(0, 2, 3, 1, 4)
    gt_g = gt[:, :3 * h].reshape(b, s, g, 3 * r).transpose(0, 2, 1, 3)
    gt_g = jnp.pad(gt_g, ((0, 0), (0, 0), (0, 0), (0, LANES - 3 * r)))
    ov = _overlap(nr, n_cmp, nsp, n_sel)
    o = _nsa_attn(qg, kct, vc, kst, vs, kwt, vw, gt_g, ov, n_cmp, tq)
    o = o.transpose(0, 3, 1, 2, 4).reshape(b * s, MIX_WIDTH)
    y_p = _out_proj(o, z, xp2, w_out, _row_tile(b * s)).reshape(b, s, d)
    win_len = min(WINDOW, s)
    win_out_p = win_p.reshape(b, s, 2, g, hd)[:, s - win_len:]

    xs2 = x_s.reshape(db * ds, d)
    qs, rows_s, win_s, gts, zs = project(xs2, jnp.tile(p_len + jnp.arange(ds, dtype=jnp.int32), db))
    t_all = p_len + ds
    nr_s = p_len // CMP_STRIDE
    n_cmp_s = (t_all - CMP_LEN) // CMP_STRIDE + 1
    assert n_cmp_s <= nr_s - 1
    n_sel_s = -(-t_all // SEL_BLOCK)
    nsp_s = -(-n_sel_s // LANES) * LANES
    cache_view = cache.reshape(cache.shape[0], PAGE_SIZE, 4 * KV_WIDTH)
    kvc_s = _page_gather(page_table, cache_view, 0, 2 * KV_WIDTH)
    x16_s = kvc_s.reshape(db, nr_s, CMP_STRIDE, 2, g, hd).transpose(3, 0, 4, 1, 2, 5)
    x16_s = x16_s.reshape(2, db * g, nr_s, CMP_STRIDE * hd)
    cmp_s = _compress(x16_s, pe_flat, cmp_w1, cmp_b1, cmp_w2, k_g[0]).astype(BF16).reshape(2, db, g, nr_s, hd)
    cmp_s = cmp_s.transpose(0, 1, 3, 2, 4).reshape(2, db, nr_s, KV_WIDTH)
    n_rows = r * ds * g
    q_rows = (qs * SCALE).reshape(db, ds, g, r, hd).transpose(0, 3, 1, 2, 4).reshape(db, n_rows, 1, hd)
    group_of_row = jnp.arange(n_rows) % g
    diag = (group_of_row[:, None] == jnp.arange(g)[None, :]).astype(F32)
    qbd = (q_rows * diag[None, :, :, None]).astype(BF16).reshape(db, n_rows, KV_WIDTH)
    rows_s3 = rows_s.astype(BF16).reshape(db, ds, 4, KV_WIDTH)
    win_s3 = win_s.astype(BF16).reshape(db, ds, 2, KV_WIDTH)
    new = lambda a: _pad_rows(a, NEW_PAD)
    gts_r = gts[:, :3 * h].reshape(db, ds, g, r, 3).transpose(0, 3, 1, 2, 4).reshape(db, n_rows, 3)
    gts_r = jnp.pad(gts_r, ((0, 0), (0, 0), (0, LANES - 3)))
    ov_s = _overlap(nr_s, n_cmp_s, nsp_s, n_sel_s)
    win_view = win_past.reshape(db, wb, 2 * KV_WIDTH)
    o_full = _nsa_sample(page_table, cache_view, qbd, cmp_s[0], cmp_s[1], win_view,
                         new(rows_s3[:, :, 2]), new(rows_s3[:, :, 3]), new(win_s3[:, :, 0]), new(win_s3[:, :, 1]),
                         gts_r, ov_s, ds, p_len, n_cmp_s)
    o6 = o_full.reshape(db, r, ds, g, g, hd)
    o_s = o6[:, :, :, jnp.arange(g), jnp.arange(g), :]
    o_s = o_s.transpose(0, 2, 3, 1, 4).reshape(db * ds, MIX_WIDTH)
    y_s = _out_proj(o_s, zs, xs2, w_out, db * ds).reshape(db, ds, d)
    win_new = win_s.reshape(db, ds, 2, g, hd)
    win_out_s = jnp.concatenate([win_past, win_new], axis=1)[:, ds:]
    return (y_p, y_s, rows_p.reshape(b, s, 4, g, hd), rows_s.reshape(db, ds, 4, g, hd), win_out_p, win_out_s)


def kernel(x_prompt, x_sample, cache_moba_kv, cache_nsa_kv, state_nsa_win, page_table, a_norm, a_w_in, a_q_norm,
           a_k_norm, a_w_out, b_norm, b_w_in, b_q_norm, b_k_norm, b_cmp_pe, b_cmp_w1, b_cmp_b1, b_cmp_w2, b_w_out):
    depth = a_norm.shape[0] + b_norm.shape[0]
    n_pool = cache_moba_kv.shape[1]
    moba_pool = cache_moba_kv.reshape((-1,) + cache_moba_kv.shape[2:])
    nsa_pool = cache_nsa_kv.reshape((-1,) + cache_nsa_kv.shape[2:])
    xp, xs = x_prompt, x_sample
    moba_p, moba_s, nsa_p, nsa_s, win_p, win_s = [], [], [], [], [], []
    for layer in range(depth):
        i = layer // 2
        pt = page_table + i * n_pool
        if layer % 2 == 0:
            xp, xs, kvp, kvs = _moba_layer(xp, xs, moba_pool, pt, a_norm[i], a_w_in[i], a_q_norm[i],
                                           a_k_norm[i], a_w_out[i])
            moba_p.append(kvp)
            moba_s.append(kvs)
        else:
            xp, xs, rp, rs, wp, ws = _nsa_layer(xp, xs, nsa_pool, state_nsa_win[i], pt, b_norm[i],
                                                b_w_in[i], b_q_norm[i], b_k_norm[i], b_cmp_pe[i], b_cmp_w1[i],
                                                b_cmp_b1[i], b_cmp_w2[i], b_w_out[i])
            nsa_p.append(rp)
            nsa_s.append(rs)
            win_p.append(wp)
            win_s.append(ws)
    return (xp, xs, jnp.stack(moba_p), jnp.stack(moba_s), jnp.stack(nsa_p), jnp.stack(nsa_s),
            jnp.stack(win_p), jnp.stack(win_s))
```

```python
import functools

import numpy as np
import jax
import jax.numpy as jnp
from jax import lax
from jax.experimental import pallas as pl
from jax.experimental.pallas import tpu as pltpu

F32 = jnp.float32
BF16 = jnp.bfloat16

D_MODEL = 1024
N_HEADS = 16
HEAD_DIM = 64
MIX_WIDTH = N_HEADS * HEAD_DIM
ROT_DIM = HEAD_DIM // 4
ROPE_THETA = 500000.0
NORM_EPS = 1e-6
PAGE_SIZE = 128
MOBA_BLOCK = 256
MOBA_TOPK = 3
NSA_KV_GROUPS = 4
NSA_REP = N_HEADS // NSA_KV_GROUPS
KV_WIDTH = NSA_KV_GROUPS * HEAD_DIM
CMP_LEN = 32
CMP_STRIDE = 16
CMP_HIDDEN = 2 * HEAD_DIM
SEL_BLOCK = 64
SEL_TOPN = 16
WINDOW = 512
SCALE = HEAD_DIM ** -0.5

LANES = 128
NEG = -1e30
M_FLOOR = -1e30
MASKED = -3e38
NEW_PAD = 128
VMEM_LIMIT = 48 * 1024 * 1024

PLAIN, ROPE, NORM_ROPE = 0, 1, 2
NT_DIMS = (((1,), (1,)), ((), ()))


def _cparams(sem):
    return pltpu.CompilerParams(dimension_semantics=sem, vmem_limit_bytes=VMEM_LIMIT)


def _dot(a, b):
    return jnp.dot(a, b, preferred_element_type=F32)


def _dot_nt(a, b):
    return lax.dot_general(a, b, NT_DIMS, preferred_element_type=F32)


def _split_dot(a, b):
    hi = a.astype(BF16)
    lo = (a - hi.astype(F32)).astype(BF16)
    return _dot(hi, b) + _dot(lo, b)


def _proj_kernel(x_ref, g_ref, w_ref, gain_ref, cos_ref, sa_ref, sb_ref, bd_ref, o_ref, xn_ref, *, types, tn):
    j = pl.program_id(1)

    @pl.when(j == 0)
    def _():
        x = x_ref[...]
        ms = jnp.mean(x * x, axis=-1, keepdims=True)
        xn_ref[...] = (x * lax.rsqrt(ms + NORM_EPS) * g_ref[...]).astype(BF16)

    y = _dot(xn_ref[...], w_ref[...])

    def head_norm(v):
        ss = _split_dot(v * v, bd_ref[...])
        return v * lax.rsqrt(ss * (1.0 / HEAD_DIM) + NORM_EPS) * gain_ref[...]

    def rope(v):
        rep = tn // LANES
        c = jnp.concatenate([cos_ref[...]] * rep, axis=1)
        sa = jnp.concatenate([sa_ref[...]] * rep, axis=1)
        sb = jnp.concatenate([sb_ref[...]] * rep, axis=1)
        half = ROT_DIM // 2
        return v * c + pltpu.roll(v, tn - half, 1) * sa + pltpu.roll(v, half, 1) * sb

    def emit(t):
        if t == PLAIN:
            o_ref[...] = y
        elif t == ROPE:
            o_ref[...] = rope(y)
        else:
            o_ref[...] = rope(head_norm(y))

    kinds = sorted(set(types))
    if len(kinds) == 1:
        emit(kinds[0])
    else:
        for t in kinds:
            cond = functools.reduce(jnp.logical_or, [j == jj for jj, tt in enumerate(types) if tt == t])
            pl.when(cond)(functools.partial(emit, t))


def _head_block_diag(tn):
    return jnp.asarray(np.kron(np.eye(tn // HEAD_DIM), np.ones((HEAD_DIM, HEAD_DIM))), BF16)


def _proj(x2d, g, w, types, gain_cols, tabs, tm, tn=256):
    t_rows, d = x2d.shape
    n = w.shape[1]
    assert t_rows % tm == 0 and n % tn == 0 and len(types) == n // tn
    cos, sa, sb = tabs
    assert cos.shape[0] % tm == 0
    ntab = cos.shape[0] // tm
    tab_spec = pl.BlockSpec((tm, LANES), lambda i, j: (i % ntab, 0))
    return pl.pallas_call(
        functools.partial(_proj_kernel, types=tuple(types), tn=tn),
        grid=(t_rows // tm, n // tn),
        in_specs=[
            pl.BlockSpec((tm, d), lambda i, j: (i, 0)),
            pl.BlockSpec((1, d), lambda i, j: (0, 0)),
            pl.BlockSpec((d, tn), lambda i, j: (0, j)),
            pl.BlockSpec((1, tn), lambda i, j: (0, j)),
            tab_spec, tab_spec, tab_spec,
            pl.BlockSpec((tn, tn), lambda i, j: (0, 0)),
        ],
        out_specs=pl.BlockSpec((tm, tn), lambda i, j: (i, j)),
        out_shape=jax.ShapeDtypeStruct((t_rows, n), F32),
        scratch_shapes=[pltpu.VMEM((tm, d), BF16)],
        compiler_params=_cparams(("parallel", "arbitrary")),
        name="rmsnorm_in_proj",
    )(x2d, g.reshape(1, d), w.astype(BF16), gain_cols.reshape(1, n).astype(F32), cos, sa, sb,
      _head_block_diag(tn))


def _rope_tables(pos):
    half = ROT_DIM // 2
    inv_freq = jnp.float32(ROPE_THETA) ** (-jnp.arange(half, dtype=F32) / half)
    ang = pos.astype(F32)[:, None] * inv_freq[None, :]
    cos, sin = jnp.cos(ang), jnp.sin(ang)
    t = pos.shape[0]
    rest = HEAD_DIM - ROT_DIM
    z8, zr, one_r = jnp.zeros((t, half), F32), jnp.zeros((t, rest), F32), jnp.ones((t, rest), F32)
    c = jnp.concatenate([cos, cos, one_r], axis=1)
    sa = jnp.concatenate([-sin, z8, zr], axis=1)
    sb = jnp.concatenate([z8, sin, zr], axis=1)
    rep = LANES // HEAD_DIM
    return tuple(jnp.concatenate([a] * rep, axis=1) for a in (c, sa, sb))


def _out_proj_kernel(o_ref, z_ref, x_ref, w_ref, y_ref):
    z = z_ref[...]
    gated = o_ref[...] * (z * (1.0 / (1.0 + jnp.exp(-z))))
    y_ref[...] = x_ref[...] + _dot(gated.astype(BF16), w_ref[...])


def _out_proj(o2d, z2d, x2d, w_out, tm):
    t_rows, d = x2d.shape
    k = o2d.shape[1]
    row = lambda i: (i, 0)
    return pl.pallas_call(
        _out_proj_kernel,
        grid=(t_rows // tm,),
        in_specs=[pl.BlockSpec((tm, k), row), pl.BlockSpec((tm, k), row), pl.BlockSpec((tm, d), row),
                  pl.BlockSpec((k, d), lambda i: (0, 0))],
        out_specs=pl.BlockSpec((tm, d), row),
        out_shape=jax.ShapeDtypeStruct((t_rows, d), F32),
        compiler_params=_cparams(("parallel",)),
        name="gated_out_proj",
    )(o2d, z2d, x2d, w_out.astype(BF16))


def _top_k_mask(scores, lane, k, width):
    sel = jnp.zeros(scores.shape, F32)
    lane = lane.astype(F32)
    for _ in range(k):
        mx = jnp.max(scores, axis=-1, keepdims=True)
        first = jnp.min(jnp.where(scores == mx, lane, float(width)), axis=-1, keepdims=True)
        hit = lane == first
        sel = jnp.where(hit & (mx > -jnp.inf), 1.0, sel)
        scores = jnp.where(hit, -jnp.inf, scores)
    return sel


def _online_update(s, mask, v, m_ref, l_ref, acc_ref):
    s = jnp.where(mask, s, NEG)
    m_old = m_ref[...]
    m_new = jnp.maximum(m_old, jnp.max(s, axis=-1, keepdims=True))
    alpha = jnp.exp(m_old - m_new)
    p = jnp.where(mask, jnp.exp(s - m_new), 0.0)
    l_ref[...] = alpha * l_ref[...] + jnp.sum(p, axis=-1, keepdims=True)
    acc_ref[...] = alpha * acc_ref[...] + _dot(p.astype(BF16), v)
    m_ref[...] = m_new


def _reset(m_ref, l_ref, acc_ref):
    m_ref[...] = jnp.full(m_ref.shape, NEG, F32)
    l_ref[...] = jnp.zeros(l_ref.shape, F32)
    acc_ref[...] = jnp.zeros(acc_ref.shape, F32)


def _block_mean_kernel(k_ref, o_ref):
    o_ref[0, 0] = jnp.sum(k_ref[0], axis=0, keepdims=True) * (1.0 / MOBA_BLOCK)


def _block_mean(kv3):
    b, s, _ = kv3.shape
    nb = s // MOBA_BLOCK
    return pl.pallas_call(
        _block_mean_kernel,
        grid=(b, nb),
        in_specs=[pl.BlockSpec((1, MOBA_BLOCK, MIX_WIDTH), lambda i, j: (i, j, 0))],
        out_specs=pl.BlockSpec((1, 1, 1, MIX_WIDTH), lambda i, j: (i, j, 0, 0)),
        out_shape=jax.ShapeDtypeStruct((b, nb, 1, MIX_WIDTH), F32),
        compiler_params=_cparams(("parallel", "parallel")),
        name="moba_block_mean",
    )(kv3)


def _top_k_mask_cols(scores, row, k, height):
    sel = jnp.zeros(scores.shape, F32)
    row = row.astype(F32)
    for _ in range(k):
        mx = jnp.max(scores, axis=0, keepdims=True)
        first = jnp.min(jnp.where(scores == mx, row, float(height)), axis=0, keepdims=True)
        hit = row == first
        sel = jnp.where(hit & (mx > -jnp.inf), 1.0, sel)
        scores = jnp.where(hit, -jnp.inf, scores)
    return sel


def _softmax_stage(s, mask, m_ref, l_ref, p_ref, a_ref, idx):
    s = jnp.where(mask, s, MASKED)
    m_old = m_ref[idx]
    m_new = jnp.maximum(m_old, jnp.max(s, axis=0, keepdims=True))
    alpha = jnp.exp(m_old - m_new)
    p = jnp.exp(s - m_new)
    l_ref[idx] = alpha * l_ref[idx] + jnp.sum(p, axis=0, keepdims=True)
    m_ref[idx] = m_new
    p_ref[idx] = p.astype(BF16)
    a_ref[idx] = alpha


def _pipelined_flash(n_past, qk, smax, pv):
    qk(0, 0)

    def pair(u, carry):
        t0 = 2 * u
        qk(t0 + 1, 1)
        pv(jnp.maximum(t0 - 1, 0), 1)
        smax(t0, 0, False)
        qk(t0 + 2, 0)
        pv(t0, 0)
        smax(t0 + 1, 1, False)
        return carry

    lax.fori_loop(0, n_past // 2, pair, 0)
    odd = n_past % 2 == 1

    @pl.when(odd)
    def _():
        t0 = n_past - 1
        qk(n_past, 1)
        pv(jnp.maximum(t0 - 1, 0), 1)
        smax(t0, 0, False)
        pv(t0, 0)
        smax(n_past, 1, True)
        pv(n_past, 1)

    @pl.when(jnp.logical_not(odd))
    def _():
        pv(jnp.maximum(n_past - 1, 0), 1)
        smax(n_past, 0, True)
        pv(n_past, 0)


def _moba_attn_kernel(qt_ref, km_ref, k_ref, vt_ref, o_ref, sel_ref, m_ref, l_ref, acc_ref,
                      s0_ref, s1_ref, p0_ref, p1_ref, a0_ref, a1_ref, *, hp):
    i = pl.program_id(2)
    blk = MOBA_BLOCK
    nb = km_ref.shape[2]
    s_buf, p_buf, a_buf = (s0_ref, s1_ref), (p0_ref, p1_ref), (a0_ref, a1_ref)
    blk_id = lax.broadcasted_iota(jnp.int32, (nb, blk), 0)
    for hh in range(hp):
        sb = jnp.where(blk_id < i, _dot(km_ref[0, hh], qt_ref[0, hh]), -jnp.inf)
        sel_ref[hh] = _top_k_mask_cols(sb, blk_id, min(MOBA_TOPK, nb), nb)
    m_ref[...] = jnp.full(m_ref.shape, M_FLOOR, F32)
    l_ref[...] = jnp.zeros(l_ref.shape, F32)
    acc_ref[...] = jnp.zeros(acc_ref.shape, F32)
    p1_ref[...] = jnp.zeros(p1_ref.shape, BF16)
    a1_ref[...] = jnp.ones(a1_ref.shape, F32)
    causal = lax.broadcasted_iota(jnp.int32, (blk, blk), 0) <= lax.broadcasted_iota(jnp.int32, (blk, blk), 1)

    def qk(t, slot):
        rows = pl.ds(pl.multiple_of(t * blk, blk), blk)
        for hh in range(hp):
            s_buf[slot][hh] = _dot(k_ref[0, hh, rows, :], qt_ref[0, hh])

    def smax(t, slot, diagonal):
        for hh in range(hp):
            mask = causal if diagonal else sel_ref[hh, pl.ds(t, 1), :] > 0.5
            _softmax_stage(s_buf[slot][hh], mask, m_ref, l_ref, p_buf[slot], a_buf[slot], hh)

    def pv(t, slot):
        for hh in range(hp):
            acc_ref[hh] = a_buf[slot][hh] * acc_ref[hh] + _dot(vt_ref[0, hh, t], p_buf[slot][hh])

    _pipelined_flash(i, qk, smax, pv)
    for hh in range(hp):
        o_ref[0, hh] = acc_ref[hh] / l_ref[hh]


def _moba_attn(qt, km, k, vt, hp=4):
    b, h, d, s = qt.shape
    nb = s // MOBA_BLOCK
    blk = MOBA_BLOCK
    assert h % hp == 0
    per_bh = lambda n: (lambda bi, hi, i: (bi, hi) + (0,) * n)
    return pl.pallas_call(
        functools.partial(_moba_attn_kernel, hp=hp),
        grid=(b, h // hp, nb),
        in_specs=[
            pl.BlockSpec((1, hp, d, blk), lambda bi, hi, i: (bi, hi, 0, i)),
            pl.BlockSpec((1, hp, nb, d), per_bh(2)),
            pl.BlockSpec((1, hp, s, d), per_bh(2)),
            pl.BlockSpec((1, hp, nb, d, blk), per_bh(3)),
        ],
        out_specs=pl.BlockSpec((1, hp, d, blk), lambda bi, hi, i: (bi, hi, 0, i)),
        out_shape=jax.ShapeDtypeStruct((b, h, d, s), F32),
        scratch_shapes=[pltpu.VMEM((hp, nb, blk), F32), pltpu.VMEM((hp, 1, blk), F32), pltpu.VMEM((hp, 1, blk), F32),
                        pltpu.VMEM((hp, d, blk), F32),
                        pltpu.VMEM((hp, blk, blk), F32), pltpu.VMEM((hp, blk, blk), F32),
                        pltpu.VMEM((hp, blk, blk), BF16), pltpu.VMEM((hp, blk, blk), BF16),
                        pltpu.VMEM((hp, 1, blk), F32), pltpu.VMEM((hp, 1, blk), F32)],
        compiler_params=_cparams(("parallel", "parallel", "arbitrary")),
        name="moba_prompt_attn",
    )(qt, km, k, vt)


def _moba_sample_kernel(pt_ref, k0_ref, k1_ref, v0_ref, v1_ref, qbd_ref, kn_ref, vn_ref, o_ref,
                        ob_ref, mb_ref, lb_ref, sc_ref, w_ref, *, n_new):
    del pt_ref
    j = pl.program_id(1)
    nbp = pl.num_programs(1)
    qbd = qbd_ref[0]
    rows = qbd.shape[0]
    s = jnp.concatenate([_dot_nt(qbd, k0_ref[0].astype(BF16)), _dot_nt(qbd, k1_ref[0].astype(BF16))], axis=1)
    sc_ref[j] = jnp.sum(s, axis=-1, keepdims=True) * (1.0 / MOBA_BLOCK)
    m = jnp.max(s, axis=-1, keepdims=True)
    p = jnp.exp(s - m)
    mb_ref[j] = m
    lb_ref[j] = jnp.sum(p, axis=-1, keepdims=True)
    pb = p.astype(BF16)
    ob_ref[j] = _dot(pb[:, :PAGE_SIZE], v0_ref[0].astype(BF16)) + _dot(pb[:, PAGE_SIZE:], v1_ref[0].astype(BF16))

    @pl.when(j == nbp - 1)
    def _():
        nb = sc_ref.shape[0]
        blk_id = lax.broadcasted_iota(jnp.int32, sc_ref.shape, 0).astype(F32)
        sc = sc_ref[...]
        sel = jnp.zeros(sc.shape, F32)
        for _ in range(MOBA_TOPK):
            mx = jnp.max(sc, axis=0, keepdims=True)
            first = jnp.min(jnp.where(sc == mx, blk_id, float(nb)), axis=0, keepdims=True)
            hit = blk_id == first
            sel = jnp.where(hit, 1.0, sel)
            sc = jnp.where(hit, -jnp.inf, sc)
        s_new = _dot_nt(qbd, kn_ref[0])
        t_row = lax.broadcasted_iota(jnp.int32, s_new.shape, 0) // N_HEADS
        t_col = lax.broadcasted_iota(jnp.int32, s_new.shape, 1)
        ok = (t_col <= t_row) & (t_col < n_new)
        m_new = jnp.max(jnp.where(ok, s_new, NEG), axis=-1, keepdims=True)
        p_new = jnp.where(ok, jnp.exp(s_new - m_new), 0.0)
        l_new = jnp.sum(p_new, axis=-1, keepdims=True)
        o_new = _dot(p_new.astype(BF16), vn_ref[0])
        picked = sel > 0.5
        m_all = jnp.maximum(jnp.max(jnp.where(picked, mb_ref[...], NEG), axis=0), m_new)
        w = jnp.where(picked, jnp.exp(mb_ref[...] - m_all[None]), 0.0)
        w_ref[...] = w
        w_new = jnp.exp(m_new - m_all)
        den = jnp.sum(w * lb_ref[...], axis=0) + w_new * l_new

        def add_block(n, acc):
            return acc + w_ref[n] * ob_ref[n]

        num = lax.fori_loop(0, nb, add_block, w_new * o_new)
        o_ref[0] = num / den


def _moba_sample(page_table, cache_view, qbd, k_new, v_new, n_new):
    db, rows, _ = qbd.shape
    n_pages = page_table.shape[1]
    ppb = MOBA_BLOCK // PAGE_SIZE
    assert ppb == 2 and n_pages % ppb == 0
    nbp = n_pages // ppb
    assert nbp >= MOBA_TOPK
    pt = page_table.reshape(-1)

    def page_spec(u, half):
        return pl.BlockSpec((1, PAGE_SIZE, MIX_WIDTH), lambda b, j, pt_ref: (pt_ref[b * n_pages + ppb * j + u], 0, half))

    per_b = lambda b, j, pt_ref: (b, 0, 0)
    grid_spec = pltpu.PrefetchScalarGridSpec(
        num_scalar_prefetch=1,
        grid=(db, nbp),
        in_specs=[page_spec(0, 0), page_spec(1, 0), page_spec(0, 1), page_spec(1, 1),
                  pl.BlockSpec((1, rows, MIX_WIDTH), per_b),
                  pl.BlockSpec((1, NEW_PAD, MIX_WIDTH), per_b),
                  pl.BlockSpec((1, NEW_PAD, MIX_WIDTH), per_b)],
        out_specs=pl.BlockSpec((1, rows, MIX_WIDTH), per_b),
        scratch_shapes=[pltpu.VMEM((nbp, rows, MIX_WIDTH), F32), pltpu.VMEM((nbp, rows, 1), F32),
                        pltpu.VMEM((nbp, rows, 1), F32), pltpu.VMEM((nbp, rows, 1), F32),
                        pltpu.VMEM((nbp, rows, 1), F32)],
    )
    return pl.pallas_call(
        functools.partial(_moba_sample_kernel, n_new=n_new),
        grid_spec=grid_spec,
        out_shape=jax.ShapeDtypeStruct((db, rows, MIX_WIDTH), F32),
        compiler_params=_cparams(("parallel", "arbitrary")),
        name="moba_sample_attn",
    )(pt, cache_view, cache_view, cache_view, cache_view, qbd, k_new, v_new)


def _compress_kernel(x_ref, pe_ref, w1_ref, b1_ref, w2_ref, kg_ref, o_ref):
    c = pl.program_id(0)
    x = x_ref[0, 0]
    nr = x.shape[0]
    half = x.shape[1]
    pe = pe_ref[0]
    p0 = _dot((x + pe[0:1]).astype(BF16), w1_ref[0, :half, :])
    p1 = _dot((x + pe[1:2]).astype(BF16), w1_ref[0, half:, :])
    h = p0 + pltpu.roll(p1, nr - 1, 0) + b1_ref[0]
    g = 0.5 * h * (1.0 + jnp.tanh(0.7978845608028654 * (h + 0.044715 * (h * h * h))))
    y = _dot(g.astype(BF16), w2_ref[0])
    yn = y * lax.rsqrt(jnp.mean(y * y, axis=-1, keepdims=True) + NORM_EPS) * kg_ref[...]
    y = jnp.where(c == 0, yn, y)
    row = lax.broadcasted_iota(jnp.int32, y.shape, 0)
    o_ref[0, 0] = jnp.where(row < nr - 1, y, 0.0)


def _compress(x16, pe, w1, b1, w2, kg0):
    _, bb, nr, width = x16.shape
    per_c = lambda c, i: (c, 0, 0)
    return pl.pallas_call(
        _compress_kernel,
        grid=(2, bb),
        in_specs=[
            pl.BlockSpec((1, 1, nr, width), lambda c, i: (c, i, 0, 0)),
            pl.BlockSpec((1, 2, width), per_c),
            pl.BlockSpec((1, CMP_LEN * HEAD_DIM, CMP_HIDDEN), per_c),
            pl.BlockSpec((1, 1, CMP_HIDDEN), per_c),
            pl.BlockSpec((1, CMP_HIDDEN, HEAD_DIM), per_c),
            pl.BlockSpec((1, HEAD_DIM), lambda c, i: (0, 0)),
        ],
        out_specs=pl.BlockSpec((1, 1, nr, HEAD_DIM), lambda c, i: (c, i, 0, 0)),
        out_shape=jax.ShapeDtypeStruct((2, bb, nr, HEAD_DIM), F32),
        compiler_params=_cparams(("parallel", "parallel")),
        name="nsa_compress",
    )(x16, pe.reshape(2, 2, width), w1.astype(BF16), b1.reshape(2, 1, CMP_HIDDEN), w2.astype(BF16),
      kg0.reshape(1, HEAD_DIM))


def _page_gather_kernel(pt_ref, *refs, n_in):
    del pt_ref
    o_ref = refs[n_in]
    for u in range(n_in):
        o_ref[0, u * PAGE_SIZE:(u + 1) * PAGE_SIZE, :] = refs[u][0]


def _page_gather(page_table, cache_view, lane_block, width, pages_per_step=8):
    db, n_pages = page_table.shape
    assert n_pages % pages_per_step == 0
    pt = page_table.reshape(-1)

    def page_spec(u):
        return pl.BlockSpec((1, PAGE_SIZE, width),
                            lambda b, p, pt_ref: (pt_ref[b * n_pages + p * pages_per_step + u], 0, lane_block))

    grid_spec = pltpu.PrefetchScalarGridSpec(
        num_scalar_prefetch=1,
        grid=(db, n_pages // pages_per_step),
        in_specs=[page_spec(u) for u in range(pages_per_step)],
        out_specs=pl.BlockSpec((1, pages_per_step * PAGE_SIZE, width), lambda b, p, pt_ref: (b, p, 0)),
    )
    return pl.pallas_call(
        functools.partial(_page_gather_kernel, n_in=pages_per_step),
        grid_spec=grid_spec,
        out_shape=jax.ShapeDtypeStruct((db, n_pages * PAGE_SIZE, width), cache_view.dtype),
        compiler_params=_cparams(("parallel", "parallel")),
        name="nsa_page_gather",
    )(pt, *([cache_view] * pages_per_step))


def _overlap(n_rows, n_cmp, n_cols, n_sel):
    i = np.arange(n_rows)[:, None]
    j = np.arange(n_cols)[None, :]
    st = i * CMP_STRIDE
    j0 = j * SEL_BLOCK
    ov = (st < j0 + SEL_BLOCK) & (st + CMP_LEN > j0) & (i < n_cmp) & (j < n_sel)
    return jnp.asarray(ov, BF16)


def _sigmoid(x):
    return 1.0 / (1.0 + jnp.exp(-x))


def _nsa_select(p_sum, ov, pos, n_cols):
    imp = _split_dot(p_sum, ov)
    j = lax.broadcasted_iota(jnp.int32, imp.shape, 1)
    cur = pos // SEL_BLOCK
    forced = (j == 0) | (j == cur) | (j == cur - 1)
    imp = jnp.where(forced, jnp.inf, jnp.where(j <= cur, imp, -jnp.inf))
    return _top_k_mask(imp, j, min(SEL_TOPN, n_cols), n_cols)


def _masked_softmax(s, mask):
    s = jnp.where(mask, s, NEG)
    m = jnp.max(s, axis=-1, keepdims=True)
    p = jnp.where(mask, jnp.exp(s - m), 0.0)
    return p / jnp.maximum(jnp.sum(p, axis=-1, keepdims=True), 1e-30)


def _nsa_attn_kernel(qt_ref, kc_ref, vct_ref, ks_ref, vst_ref, kw_ref, vwt_ref, gt_ref, ovt_ref, o_ref,
                     sel_ref, m_ref, l_ref, acc_ref, s0_ref, s1_ref, p0_ref, p1_ref, a0_ref, a1_ref, *, tq, n_cmp):
    i = pl.program_id(2)
    r = NSA_REP
    qt = qt_ref[0, 0, 0]
    pos = i * tq + lax.broadcasted_iota(jnp.int32, (1, tq), 1)
    key_id = lax.broadcasted_iota(jnp.int32, (tq, tq), 0)
    q_id = lax.broadcasted_iota(jnp.int32, (tq, tq), 1)
    causal = key_id <= q_id
    cols = [slice(rr * tq, (rr + 1) * tq) for rr in range(r)]

    ncp = kc_ref.shape[2]
    n_id = lax.broadcasted_iota(jnp.int32, (ncp, tq), 0)
    cmask = (n_id * CMP_STRIDE + (CMP_LEN - 1) <= pos) & (n_id < n_cmp)
    sc = _dot(kc_ref[0, 0], qt)
    p_sum = jnp.zeros((ncp, tq), F32)
    pcs = []
    for rr in range(r):
        s = jnp.where(cmask, sc[:, cols[rr]], MASKED)
        m = jnp.maximum(jnp.max(s, axis=0, keepdims=True), M_FLOOR)
        p = jnp.exp(s - m)
        pc = p * (1.0 / jnp.maximum(jnp.sum(p, axis=0, keepdims=True), 1e-30))
        p_sum = p_sum + pc
        pcs.append(pc.astype(BF16))
    o_c = _dot(vct_ref[0, 0], jnp.concatenate(pcs, axis=1))

    nsp = ovt_ref.shape[0]
    hi = p_sum.astype(BF16)
    lo = (p_sum - hi.astype(F32)).astype(BF16)
    imp = _dot(ovt_ref[...], hi) + _dot(ovt_ref[...], lo)
    j = lax.broadcasted_iota(jnp.int32, (nsp, tq), 0)
    cur = pos // SEL_BLOCK
    forced = (j == 0) | (j == cur) | (j == cur - 1)
    imp = jnp.where(forced, jnp.inf, jnp.where(j <= cur, imp, -jnp.inf))
    sel_ref[...] = _top_k_mask_cols(imp, j, min(SEL_TOPN, nsp), nsp)

    w_tiles = []
    for back, wmask in ((2, key_id > q_id), (1, None), (0, causal)):
        live = i >= back
        wt = jnp.maximum(i - back, 0)
        mask = live if wmask is None else wmask & live
        w_tiles.append((wt, mask, _dot(kw_ref[0, 0, wt], qt)))
    ow_parts = []
    for rr in range(r):
        ss = [jnp.where(mask, s[:, cols[rr]], MASKED) for _, mask, s in w_tiles]
        m = ss[0].max(axis=0, keepdims=True)
        for s in ss[1:]:
            m = jnp.maximum(m, s.max(axis=0, keepdims=True))
        ps = [jnp.exp(s - m) for s in ss]
        den = sum(p.sum(axis=0, keepdims=True) for p in ps)
        num = sum(_dot(vwt_ref[0, 0, wt], p.astype(BF16)) for (wt, _, _), p in zip(w_tiles, ps))
        ow_parts.append(num / den)

    s_buf, p_buf, a_buf = (s0_ref, s1_ref), (p0_ref, p1_ref), (a0_ref, a1_ref)
    m_ref[...] = jnp.full(m_ref.shape, M_FLOOR, F32)
    l_ref[...] = jnp.zeros(l_ref.shape, F32)
    acc_ref[...] = jnp.zeros(acc_ref.shape, F32)
    p1_ref[...] = jnp.zeros(p1_ref.shape, BF16)
    a1_ref[...] = jnp.ones(a1_ref.shape, F32)
    blocks_per_tile = tq // SEL_BLOCK

    def qk(t, slot):
        s_buf[slot][...] = _dot(ks_ref[0, 0, t], qt)

    def smax(t, slot, diagonal):
        rows = [jnp.broadcast_to(sel_ref[pl.ds(t * blocks_per_tile + u, 1), :], (SEL_BLOCK, tq))
                for u in range(blocks_per_tile)]
        mask = jnp.concatenate(rows, axis=0) > 0.5
        if diagonal:
            mask = mask & causal
        for rr in range(r):
            idx = (slice(None), cols[rr])
            _softmax_stage(s_buf[slot][idx], mask, m_ref, l_ref, p_buf[slot], a_buf[slot], idx)

    def pv(t, slot):
        acc_ref[...] = a_buf[slot][...] * acc_ref[...] + _dot(vst_ref[0, 0, t], p_buf[slot][...])

    _pipelined_flash(i, qk, smax, pv)
    o_s = acc_ref[...] / l_ref[...]

    gates = _sigmoid(gt_ref[0, 0])
    for rr in range(r):
        o_ref[0, 0, rr] = (gates[3 * rr:3 * rr + 1] * o_c[:, cols[rr]] + gates[3 * rr + 1:3 * rr + 2] * o_s[:, cols[rr]]
                           + gates[3 * rr + 2:3 * rr + 3] * ow_parts[rr])


def _nsa_attn(qt, kc, vct, ks, vst, kw, vwt, gt, ovt, n_cmp):
    b, g, nq, d, w = qt.shape
    r = NSA_REP
    tq = w // r
    s = nq * tq
    ncp = kc.shape[2]
    nsp = ovt.shape[0]
    assert ks.shape[2:] == (nq, tq, d) and kw.shape[2:] == (nq, tq, d) and WINDOW == 2 * tq and tq % SEL_BLOCK == 0
    per_bg = lambda n: (lambda bi, gi, i: (bi, gi) + (0,) * n)
    return pl.pallas_call(
        functools.partial(_nsa_attn_kernel, tq=tq, n_cmp=n_cmp),
        grid=(b, g, nq),
        in_specs=[
            pl.BlockSpec((1, 1, 1, d, w), lambda bi, gi, i: (bi, gi, i, 0, 0)),
            pl.BlockSpec((1, 1, ncp, d), per_bg(2)),
            pl.BlockSpec((1, 1, d, ncp), per_bg(2)),
            pl.BlockSpec((1, 1, nq, tq, d), per_bg(3)),
            pl.BlockSpec((1, 1, nq, d, tq), per_bg(3)),
            pl.BlockSpec((1, 1, nq, tq, d), per_bg(3)),
            pl.BlockSpec((1, 1, nq, d, tq), per_bg(3)),
            pl.BlockSpec((1, 1, 16, tq), lambda bi, gi, i: (bi, gi, 0, i)),
            pl.BlockSpec((nsp, ncp), lambda bi, gi, i: (0, 0)),
        ],
        out_specs=pl.BlockSpec((1, 1, r, d, tq), lambda bi, gi, i: (bi, gi, 0, 0, i)),
        out_shape=jax.ShapeDtypeStruct((b, g, r, d, s), F32),
        scratch_shapes=[pltpu.VMEM((nsp, tq), F32), pltpu.VMEM((1, w), F32), pltpu.VMEM((1, w), F32),
                        pltpu.VMEM((d, w), F32),
                        pltpu.VMEM((tq, w), F32), pltpu.VMEM((tq, w), F32),
                        pltpu.VMEM((tq, w), BF16), pltpu.VMEM((tq, w), BF16),
                        pltpu.VMEM((1, w), F32), pltpu.VMEM((1, w), F32)],
        compiler_params=_cparams(("parallel", "parallel", "arbitrary")),
        name="nsa_prompt_attn",
    )(qt, kc, vct, ks, vst, kw, vwt, gt, ovt)


def _nsa_sample_kernel(pt_ref, pg_ref, qbd_ref, kc_ref, vc_ref, win_ref, ksn_ref, vsn_ref, kwn_ref, vwn_ref,
                       gt_ref, ov_ref, o_ref, sel_ref, oc_ref, m_ref, l_ref, acc_ref, *, n_new, p_len, n_cmp):
    del pt_ref
    p = pl.program_id(1)
    n_pg = pl.num_programs(1)
    qbd = qbd_ref[0]
    rows = qbd.shape[0]
    tg = rows // NSA_REP
    t_of_row = (lax.broadcasted_iota(jnp.int32, (rows, 1), 0) % tg) // NSA_KV_GROUPS
    pos = p_len + t_of_row

    @pl.when(p == 0)
    def _():
        ncp = kc_ref.shape[1]
        n_id = lax.broadcasted_iota(jnp.int32, (rows, ncp), 1)
        cmask = (n_id * CMP_STRIDE + (CMP_LEN - 1) <= pos) & (n_id < n_cmp)
        pc = _masked_softmax(_dot_nt(qbd, kc_ref[0]), cmask)
        oc_ref[...] = _dot(pc.astype(BF16), vc_ref[0])
        p_sum = pc[0:tg]
        for rr in range(1, NSA_REP):
            p_sum = p_sum + pc[rr * tg:(rr + 1) * tg]
        sel = _nsa_select(p_sum, ov_ref[...], pos[0:tg], ov_ref.shape[-1])
        sel_ref[...] = jnp.concatenate([sel] * NSA_REP, axis=0)
        _reset(m_ref, l_ref, acc_ref)

    page = pg_ref[0]
    blocks_per_page = PAGE_SIZE // SEL_BLOCK
    lane = lax.broadcasted_iota(jnp.int32, sel_ref.shape, 1)
    key = lax.broadcasted_iota(jnp.int32, (rows, PAGE_SIZE), 1)
    chosen = jnp.zeros((rows, PAGE_SIZE), F32)
    for u in range(blocks_per_page):
        col = jnp.sum(jnp.where(lane == p * blocks_per_page + u, sel_ref[...], 0.0), axis=-1, keepdims=True)
        chosen = jnp.where(key // SEL_BLOCK == u, col, chosen)
    _online_update(_dot_nt(qbd, page[:, :KV_WIDTH].astype(BF16)), chosen > 0.5, page[:, KV_WIDTH:].astype(BF16),
                   m_ref, l_ref, acc_ref)

    @pl.when(p == n_pg - 1)
    def _():
        t_col = lax.broadcasted_iota(jnp.int32, (rows, NEW_PAD), 1)
        new_ok = (t_col <= t_of_row) & (t_col < n_new)
        _online_update(_dot_nt(qbd, ksn_ref[0]), new_ok, vsn_ref[0], m_ref, l_ref, acc_ref)
        o_s = acc_ref[...] / l_ref[...]
        _reset(m_ref, l_ref, acc_ref)
        win = win_ref[0]
        wb = win.shape[0]
        kw_pos = p_len - wb + lax.broadcasted_iota(jnp.int32, (rows, wb), 1)
        w_ok = (kw_pos <= pos) & (kw_pos > pos - WINDOW) & (kw_pos >= 0)
        _online_update(_dot_nt(qbd, win[:, :KV_WIDTH].astype(BF16)), w_ok, win[:, KV_WIDTH:].astype(BF16),
                       m_ref, l_ref, acc_ref)
        _online_update(_dot_nt(qbd, kwn_ref[0]), new_ok, vwn_ref[0], m_ref, l_ref, acc_ref)
        o_w = acc_ref[...] / l_ref[...]
        gates = _sigmoid(gt_ref[0])
        o_ref[0] = gates[:, 0:1] * oc_ref[...] + gates[:, 1:2] * o_s + gates[:, 2:3] * o_w


def _nsa_sample(page_table, cache_view, qbd, kc, vc, win, ks_new, vs_new, kw_new, vw_new, gt, ov, n_new, p_len, n_cmp):
    db, rows, _ = qbd.shape
    n_pages = page_table.shape[1]
    ncp = kc.shape[1]
    wb = win.shape[1]
    nsp = ov.shape[-1]
    pt = page_table.reshape(-1)
    per_b = lambda b, p, pt_ref: (b, 0, 0)
    new_spec = pl.BlockSpec((1, NEW_PAD, KV_WIDTH), per_b)
    grid_spec = pltpu.PrefetchScalarGridSpec(
        num_scalar_prefetch=1,
        grid=(db, n_pages),
        in_specs=[
            pl.BlockSpec((1, PAGE_SIZE, 2 * KV_WIDTH), lambda b, p, pt_ref: (pt_ref[b * n_pages + p], 0, 1)),
            pl.BlockSpec((1, rows, KV_WIDTH), per_b),
            pl.BlockSpec((1, ncp, KV_WIDTH), per_b),
            pl.BlockSpec((1, ncp, KV_WIDTH), per_b),
            pl.BlockSpec((1, wb, 2 * KV_WIDTH), per_b),
            new_spec, new_spec, new_spec, new_spec,
            pl.BlockSpec((1, rows, LANES), per_b),
            pl.BlockSpec((ncp, nsp), lambda b, p, pt_ref: (0, 0)),
        ],
        out_specs=pl.BlockSpec((1, rows, KV_WIDTH), per_b),
        scratch_shapes=[pltpu.VMEM((rows, nsp), F32), pltpu.VMEM((rows, KV_WIDTH), F32),
                        pltpu.VMEM((rows, 1), F32), pltpu.VMEM((rows, 1), F32), pltpu.VMEM((rows, KV_WIDTH), F32)],
    )
    return pl.pallas_call(
        functools.partial(_nsa_sample_kernel, n_new=n_new, p_len=p_len, n_cmp=n_cmp),
        grid_spec=grid_spec,
        out_shape=jax.ShapeDtypeStruct((db, rows, KV_WIDTH), F32),
        compiler_params=_cparams(("parallel", "arbitrary")),
        name="nsa_sample_attn",
    )(pt, cache_view, qbd, kc, vc, win, ks_new, vs_new, kw_new, vw_new, gt, ov)


def _row_tile(t_rows):
    return 512 if t_rows % 512 == 0 else t_rows


def _pad_rows(a, n):
    return jnp.pad(a, ((0, 0), (0, n - a.shape[1]), (0, 0)))


def _moba_layer(x_p, x_s, cache, page_table, norm_g, w_in, q_g, k_g, w_out):
    b, s, d = x_p.shape
    db, ds, _ = x_s.shape
    h, hd = N_HEADS, HEAD_DIM
    n_pages = page_table.shape[1]
    p_len = n_pages * PAGE_SIZE
    assert s % MOBA_BLOCK == 0 and p_len % MOBA_BLOCK == 0 and ds <= MOBA_BLOCK
    wq, wkv, wz = w_in[:, :MIX_WIDTH], w_in[:, MIX_WIDTH:3 * MIX_WIDTH], w_in[:, 3 * MIX_WIDTH:]
    gq = jnp.tile(q_g, h)
    gkv = jnp.concatenate([jnp.tile(k_g, h), jnp.ones((MIX_WIDTH,), F32)])
    ones = jnp.ones((MIX_WIDTH,), F32)

    def project(x2d, pos):
        tabs = _rope_tables(pos)
        tm = _row_tile(x2d.shape[0])
        tabs = tabs if pos.shape[0] % tm == 0 else tuple(jnp.tile(t, (tm // pos.shape[0], 1)) for t in tabs)
        q = _proj(x2d, norm_g, wq, [NORM_ROPE] * 4, gq, tabs, tm)
        kv = _proj(x2d, norm_g, wkv, [NORM_ROPE] * 4 + [PLAIN] * 4, gkv, tabs, tm)
        z = _proj(x2d, norm_g, wz, [PLAIN] * 4, ones, tabs, tm)
        return q, kv, z

    xp2 = x_p.reshape(b * s, d)
    q, kv, z = project(xp2, jnp.arange(s, dtype=jnp.int32))
    nb = s // MOBA_BLOCK
    kv3 = kv.reshape(b, s, 2 * MIX_WIDTH)
    km = _block_mean(kv3).reshape(b, nb, h, hd).transpose(0, 2, 1, 3).astype(BF16)
    qt = (q * SCALE).astype(BF16).reshape(b, s, h, hd).transpose(0, 2, 3, 1)
    kv5 = kv.astype(BF16).reshape(b, nb, MOBA_BLOCK, 2, h, hd)
    kk = kv5[:, :, :, 0].transpose(0, 3, 1, 2, 4).reshape(b, h, s, hd)
    vt = kv5[:, :, :, 1].transpose(0, 3, 1, 4, 2)
    o = _moba_attn(qt, km, kk, vt).transpose(0, 3, 1, 2).reshape(b * s, MIX_WIDTH)
    y_p = _out_proj(o, z, xp2, w_out, _row_tile(b * s)).reshape(b, s, d)

    xs2 = x_s.reshape(db * ds, d)
    qs, kvs, zs = project(xs2, jnp.tile(p_len + jnp.arange(ds, dtype=jnp.int32), db))
    rows = ds * h
    q_rows = (qs * SCALE).reshape(db, rows, 1, hd)
    head_of_row = jnp.arange(rows) % h
    diag = (head_of_row[:, None] == jnp.arange(h)[None, :]).astype(F32)
    qbd = (q_rows * diag[None, :, :, None]).astype(BF16).reshape(db, rows, MIX_WIDTH)
    kvs3 = kvs.astype(BF16).reshape(db, ds, 2 * MIX_WIDTH)
    k_new = _pad_rows(kvs3[:, :, :MIX_WIDTH], NEW_PAD)
    v_new = _pad_rows(kvs3[:, :, MIX_WIDTH:], NEW_PAD)
    cache_view = cache.astype(BF16).reshape(cache.shape[0], PAGE_SIZE, 2 * MIX_WIDTH)
    o_full = _moba_sample(page_table, cache_view, qbd, k_new, v_new, ds)
    o6 = o_full.reshape(db, ds, h, h, hd)
    o_s = o6[:, :, jnp.arange(h), jnp.arange(h), :].reshape(db * ds, MIX_WIDTH)
    y_s = _out_proj(o_s, zs, xs2, w_out, db * ds).reshape(db, ds, d)
    return (y_p, y_s, kv.reshape(b, s, 2, h, hd), kvs.reshape(db, ds, 2, h, hd))


def _nsa_layer(x_p, x_s, cache, win_past, page_table, norm_g, w_in, q_g, k_g, cmp_pe, cmp_w1, cmp_b1, cmp_w2, w_out):
    b, s, d = x_p.shape
    db, ds, _ = x_s.shape
    h, hd, g, r = N_HEADS, HEAD_DIM, NSA_KV_GROUPS, NSA_REP
    n_pages = page_table.shape[1]
    p_len = n_pages * PAGE_SIZE
    wb = win_past.shape[1]
    tq = WINDOW // 2
    assert s % tq == 0 and s % CMP_STRIDE == 0 and ds <= SEL_BLOCK and p_len % SEL_BLOCK == 0

    o_q, o_rows, o_win = MIX_WIDTH, MIX_WIDTH + 4 * KV_WIDTH, MIX_WIDTH + 6 * KV_WIDTH
    o_z = o_win + 3 * h
    wq, wrows, wwin = w_in[:, :o_q], w_in[:, o_q:o_rows], w_in[:, o_rows:o_win]
    wgt = jnp.pad(w_in[:, o_win:o_z], ((0, 0), (0, LANES - 3 * h)))
    wz = w_in[:, o_z:]
    ones_kv = jnp.ones((KV_WIDTH,), F32)
    gq = jnp.tile(q_g, h)
    grows = jnp.concatenate([ones_kv, ones_kv, jnp.tile(k_g[1], g), ones_kv])
    gwin = jnp.concatenate([jnp.tile(k_g[2], g), ones_kv])

    def project(x2d, pos):
        tabs = _rope_tables(pos)
        tm = _row_tile(x2d.shape[0])
        tabs = tabs if pos.shape[0] % tm == 0 else tuple(jnp.tile(t, (tm // pos.shape[0], 1)) for t in tabs)
        q = _proj(x2d, norm_g, wq, [NORM_ROPE] * 4, gq, tabs, tm)
        rows_ = _proj(x2d, norm_g, wrows, [ROPE, PLAIN, NORM_ROPE, PLAIN], grows, tabs, tm)
        win_ = _proj(x2d, norm_g, wwin, [NORM_ROPE, PLAIN], gwin, tabs, tm)
        gt_ = _proj(x2d, norm_g, wgt, [PLAIN], jnp.ones((LANES,), F32), tabs, tm, tn=LANES)
        z = _proj(x2d, norm_g, wz, [PLAIN] * 4, jnp.ones((MIX_WIDTH,), F32), tabs, tm)
        return q, rows_, win_, gt_, z

    pe_flat = cmp_pe.reshape(2, 2, CMP_STRIDE * hd)

    xp2 = x_p.reshape(b * s, d)
    q, rows_p, win_p, gt, z = project(xp2, jnp.arange(s, dtype=jnp.int32))
    nr = s // CMP_STRIDE
    n_cmp = (s - CMP_LEN) // CMP_STRIDE + 1
    n_sel = -(-s // SEL_BLOCK)
    nsp = -(-n_sel // LANES) * LANES
    rows6 = rows_p.reshape(b, nr, CMP_STRIDE, 4, g, hd)
    x16 = rows6[:, :, :, :2].transpose(3, 0, 4, 1, 2, 5).reshape(2, b * g, nr, CMP_STRIDE * hd)
    cmp_kv = _compress(x16, pe_flat, cmp_w1, cmp_b1, cmp_w2, k_g[0]).astype(BF16).reshape(2, b, g, nr, hd)
    kc = cmp_kv[0]
    vct = cmp_kv[1].transpose(0, 1, 3, 2)
    nq = s // tq
    rows_bf = rows_p.astype(BF16).reshape(b, nq, tq, 4, g, hd)
    ks = rows_bf[:, :, :, 2].transpose(0, 3, 1, 2, 4)
    vst = rows_bf[:, :, :, 3].transpose(0, 3, 1, 4, 2)
    win_bf = win_p.astype(BF16).reshape(b, nq, tq, 2, g, hd)
    kw = win_bf[:, :, :, 0].transpose(0, 3, 1, 2, 4)
    vwt = win_bf[:, :, :, 1].transpose(0, 3, 1, 4, 2)
    qt = (q * SCALE).astype(BF16).reshape(b, nq, tq, g, r, hd).transpose(0, 3, 1, 5, 4, 2)
    qt = qt.reshape(b, g, nq, hd, r * tq)
    gt_g = gt[:, :3 * h].reshape(b, s, g, 3 * r).transpose(0, 2, 3, 1)
    gt_g = jnp.pad(gt_g, ((0, 0), (0, 0), (0, 16 - 3 * r), (0, 0)))
    ovt = _overlap(nr, n_cmp, nsp, n_sel).T
    o = _nsa_attn(qt, kc, vct, ks, vst, kw, vwt, gt_g, ovt, n_cmp)
    o = o.transpose(0, 4, 1, 2, 3).reshape(b * s, MIX_WIDTH)
    y_p = _out_proj(o, z, xp2, w_out, _row_tile(b * s)).reshape(b, s, d)
    win_len = min(WINDOW, s)
    win_out_p = win_p.reshape(b, s, 2, g, hd)[:, s - win_len:]

    xs2 = x_s.reshape(db * ds, d)
    qs, rows_s, win_s, gts, zs = project(xs2, jnp.tile(p_len + jnp.arange(ds, dtype=jnp.int32), db))
    t_all = p_len + ds
    nr_s = p_len // CMP_STRIDE
    n_cmp_s = (t_all - CMP_LEN) // CMP_STRIDE + 1
    assert n_cmp_s <= nr_s - 1
    n_sel_s = -(-t_all // SEL_BLOCK)
    nsp_s = -(-n_sel_s // LANES) * LANES
    cache_view = cache.reshape(cache.shape[0], PAGE_SIZE, 4 * KV_WIDTH)
    kvc_s = _page_gather(page_table, cache_view, 0, 2 * KV_WIDTH)
    x16_s = kvc_s.reshape(db, nr_s, CMP_STRIDE, 2, g, hd).transpose(3, 0, 4, 1, 2, 5)
    x16_s = x16_s.reshape(2, db * g, nr_s, CMP_STRIDE * hd)
    cmp_s = _compress(x16_s, pe_flat, cmp_w1, cmp_b1, cmp_w2, k_g[0]).astype(BF16).reshape(2, db, g, nr_s, hd)
    cmp_s = cmp_s.transpose(0, 1, 3, 2, 4).reshape(2, db, nr_s, KV_WIDTH)
    n_rows = r * ds * g
    q_rows = (qs * SCALE).reshape(db, ds, g, r, hd).transpose(0, 3, 1, 2, 4).reshape(db, n_rows, 1, hd)
    group_of_row = jnp.arange(n_rows) % g
    diag = (group_of_row[:, None] == jnp.arange(g)[None, :]).astype(F32)
    qbd = (q_rows * diag[None, :, :, None]).astype(BF16).reshape(db, n_rows, KV_WIDTH)
    rows_s3 = rows_s.astype(BF16).reshape(db, ds, 4, KV_WIDTH)
    win_s3 = win_s.astype(BF16).reshape(db, ds, 2, KV_WIDTH)
    new = lambda a: _pad_rows(a, NEW_PAD)
    gts_r = gts[:, :3 * h].reshape(db, ds, g, r, 3).transpose(0, 3, 1, 2, 4).reshape(db, n_rows, 3)
    gts_r = jnp.pad(gts_r, ((0, 0), (0, 0), (0, LANES - 3)))
    ov_s = _overlap(nr_s, n_cmp_s, nsp_s, n_sel_s)
    win_view = win_past.reshape(db, wb, 2 * KV_WIDTH)
    o_full = _nsa_sample(page_table, cache_view, qbd, cmp_s[0], cmp_s[1], win_view,
                         new(rows_s3[:, :, 2]), new(rows_s3[:, :, 3]), new(win_s3[:, :, 0]), new(win_s3[:, :, 1]),
                         gts_r, ov_s, ds, p_len, n_cmp_s)
    o6 = o_full.reshape(db, r, ds, g, g, hd)
    o_s = o6[:, :, :, jnp.arange(g), jnp.arange(g), :]
    o_s = o_s.transpose(0, 2, 3, 1, 4).reshape(db * ds, MIX_WIDTH)
    y_s = _out_proj(o_s, zs, xs2, w_out, db * ds).reshape(db, ds, d)
    win_new = win_s.reshape(db, ds, 2, g, hd)
    win_out_s = jnp.concatenate([win_past, win_new], axis=1)[:, ds:]
    return (y_p, y_s, rows_p.reshape(b, s, 4, g, hd), rows_s.reshape(db, ds, 4, g, hd), win_out_p, win_out_s)


def kernel(x_prompt, x_sample, cache_moba_kv, cache_nsa_kv, state_nsa_win, page_table, a_norm, a_w_in, a_q_norm,
           a_k_norm, a_w_out, b_norm, b_w_in, b_q_norm, b_k_norm, b_cmp_pe, b_cmp_w1, b_cmp_b1, b_cmp_w2, b_w_out):
    depth = a_norm.shape[0] + b_norm.shape[0]
    n_pool = cache_moba_kv.shape[1]
    moba_pool = cache_moba_kv.reshape((-1,) + cache_moba_kv.shape[2:])
    nsa_pool = cache_nsa_kv.reshape((-1,) + cache_nsa_kv.shape[2:])
    xp, xs = x_prompt, x_sample
    moba_p, moba_s, nsa_p, nsa_s, win_p, win_s = [], [], [], [], [], []
    for layer in range(depth):
        i = layer // 2
        pt = page_table + i * n_pool
        if layer % 2 == 0:
            xp, xs, kvp, kvs = _moba_layer(xp, xs, moba_pool, pt, a_norm[i], a_w_in[i], a_q_norm[i],
                                           a_k_norm[i], a_w_out[i])
            moba_p.append(kvp)
            moba_s.append(kvs)
        else:
            xp, xs, rp, rs, wp, ws = _nsa_layer(xp, xs, nsa_pool, state_nsa_win[i], pt, b_norm[i],
                                                b_w_in[i], b_q_norm[i], b_k_norm[i], b_cmp_pe[i], b_cmp_w1[i],
                                                b_cmp_b1[i], b_cmp_w2[i], b_w_out[i])
            nsa_p.append(rp)
            nsa_s.append(rs)
            win_p.append(wp)
            win_s.append(ws)
    return (xp, xs, jnp.stack(moba_p), jnp.stack(moba_s), jnp.stack(nsa_p), jnp.stack(nsa_s),
            jnp.stack(win_p), jnp.stack(win_s))
```

```python
import functools

import numpy as np
import jax
import jax.numpy as jnp
from jax import lax
from jax.experimental import pallas as pl
from jax.experimental.pallas import tpu as pltpu

F32 = jnp.float32
BF16 = jnp.bfloat16

D_MODEL = 1024
N_HEADS = 16
HEAD_DIM = 64
MIX_WIDTH = N_HEADS * HEAD_DIM
ROT_DIM = HEAD_DIM // 4
ROPE_THETA = 500000.0
NORM_EPS = 1e-6
PAGE_SIZE = 128
MOBA_BLOCK = 256
MOBA_TOPK = 3
NSA_KV_GROUPS = 4
NSA_REP = N_HEADS // NSA_KV_GROUPS
KV_WIDTH = NSA_KV_GROUPS * HEAD_DIM
CMP_LEN = 32
CMP_STRIDE = 16
CMP_HIDDEN = 2 * HEAD_DIM
SEL_BLOCK = 64
SEL_TOPN = 16
WINDOW = 512
SCALE = HEAD_DIM ** -0.5

LANES = 128
NEG = -1e30
M_FLOOR = -1e30
MASKED = -3e38
NEW_PAD = 128
VMEM_LIMIT = 48 * 1024 * 1024

PLAIN, ROPE, NORM_ROPE = 0, 1, 2
NT_DIMS = (((1,), (1,)), ((), ()))


def _cparams(sem):
    return pltpu.CompilerParams(dimension_semantics=sem, vmem_limit_bytes=VMEM_LIMIT)


def _dot(a, b):
    return jnp.dot(a, b, preferred_element_type=F32)


def _dot_nt(a, b):
    return lax.dot_general(a, b, NT_DIMS, preferred_element_type=F32)


def _split_dot(a, b):
    hi = a.astype(BF16)
    lo = (a - hi.astype(F32)).astype(BF16)
    return _dot(hi, b) + _dot(lo, b)


def _proj_kernel(x_ref, g_ref, w_ref, gain_ref, cos_ref, sa_ref, sb_ref, bd_ref, o_ref, xn_ref, *, types, tn):
    j = pl.program_id(1)

    @pl.when(j == 0)
    def _():
        x = x_ref[...]
        ms = jnp.mean(x * x, axis=-1, keepdims=True)
        xn_ref[...] = (x * lax.rsqrt(ms + NORM_EPS) * g_ref[...]).astype(BF16)

    y = _dot(xn_ref[...], w_ref[...])

    def head_norm(v):
        ss = _split_dot(v * v, bd_ref[...])
        return v * lax.rsqrt(ss * (1.0 / HEAD_DIM) + NORM_EPS) * gain_ref[...]

    def rope(v):
        rep = tn // LANES
        c = jnp.concatenate([cos_ref[...]] * rep, axis=1)
        sa = jnp.concatenate([sa_ref[...]] * rep, axis=1)
        sb = jnp.concatenate([sb_ref[...]] * rep, axis=1)
        half = ROT_DIM // 2
        return v * c + pltpu.roll(v, tn - half, 1) * sa + pltpu.roll(v, half, 1) * sb

    def emit(t):
        if t == PLAIN:
            o_ref[...] = y
        elif t == ROPE:
            o_ref[...] = rope(y)
        else:
            o_ref[...] = rope(head_norm(y))

    kinds = sorted(set(types))
    if len(kinds) == 1:
        emit(kinds[0])
    else:
        for t in kinds:
            cond = functools.reduce(jnp.logical_or, [j == jj for jj, tt in enumerate(types) if tt == t])
            pl.when(cond)(functools.partial(emit, t))


def _head_block_diag(tn):
    return jnp.asarray(np.kron(np.eye(tn // HEAD_DIM), np.ones((HEAD_DIM, HEAD_DIM))), BF16)


def _proj(x2d, g, w, types, gain_cols, tabs, tm, tn=256):
    t_rows, d = x2d.shape
    n = w.shape[1]
    assert t_rows % tm == 0 and n % tn == 0 and len(types) == n // tn
    cos, sa, sb = tabs
    assert cos.shape[0] % tm == 0
    ntab = cos.shape[0] // tm
    tab_spec = pl.BlockSpec((tm, LANES), lambda i, j: (i % ntab, 0))
    return pl.pallas_call(
        functools.partial(_proj_kernel, types=tuple(types), tn=tn),
        grid=(t_rows // tm, n // tn),
        in_specs=[
            pl.BlockSpec((tm, d), lambda i, j: (i, 0)),
            pl.BlockSpec((1, d), lambda i, j: (0, 0)),
            pl.BlockSpec((d, tn), lambda i, j: (0, j)),
            pl.BlockSpec((1, tn), lambda i, j: (0, j)),
            tab_spec, tab_spec, tab_spec,
            pl.BlockSpec((tn, tn), lambda i, j: (0, 0)),
        ],
        out_specs=pl.BlockSpec((tm, tn), lambda i, j: (i, j)),
        out_shape=jax.ShapeDtypeStruct((t_rows, n), F32),
        scratch_shapes=[pltpu.VMEM((tm, d), BF16)],
        compiler_params=_cparams(("parallel", "arbitrary")),
        name="rmsnorm_in_proj",
    )(x2d, g.reshape(1, d), w.astype(BF16), gain_cols.reshape(1, n).astype(F32), cos, sa, sb,
      _head_block_diag(tn))


def _rope_tables(pos):
    half = ROT_DIM // 2
    inv_freq = jnp.float32(ROPE_THETA) ** (-jnp.arange(half, dtype=F32) / half)
    ang = pos.astype(F32)[:, None] * inv_freq[None, :]
    cos, sin = jnp.cos(ang), jnp.sin(ang)
    t = pos.shape[0]
    rest = HEAD_DIM - ROT_DIM
    z8, zr, one_r = jnp.zeros((t, half), F32), jnp.zeros((t, rest), F32), jnp.ones((t, rest), F32)
    c = jnp.concatenate([cos, cos, one_r], axis=1)
    sa = jnp.concatenate([-sin, z8, zr], axis=1)
    sb = jnp.concatenate([z8, sin, zr], axis=1)
    rep = LANES // HEAD_DIM
    return tuple(jnp.concatenate([a] * rep, axis=1) for a in (c, sa, sb))


def _out_proj_kernel(o_ref, z_ref, x_ref, w_ref, y_ref):
    z = z_ref[...]
    gated = o_ref[...] * (z * (1.0 / (1.0 + jnp.exp(-z))))
    y_ref[...] = x_ref[...] + _dot(gated.astype(BF16), w_ref[...])


def _out_proj(o2d, z2d, x2d, w_out, tm):
    t_rows, d = x2d.shape
    k = o2d.shape[1]
    row = lambda i: (i, 0)
    return pl.pallas_call(
        _out_proj_kernel,
        grid=(t_rows // tm,),
        in_specs=[pl.BlockSpec((tm, k), row), pl.BlockSpec((tm, k), row), pl.BlockSpec((tm, d), row),
                  pl.BlockSpec((k, d), lambda i: (0, 0))],
        out_specs=pl.BlockSpec((tm, d), row),
        out_shape=jax.ShapeDtypeStruct((t_rows, d), F32),
        compiler_params=_cparams(("parallel",)),
        name="gated_out_proj",
    )(o2d, z2d, x2d, w_out.astype(BF16))


def _top_k_mask(scores, lane, k, width):
    sel = jnp.zeros(scores.shape, F32)
    lane = lane.astype(F32)
    for _ in range(k):
        mx = jnp.max(scores, axis=-1, keepdims=True)
        first = jnp.min(jnp.where(scores == mx, lane, float(width)), axis=-1, keepdims=True)
        hit = lane == first
        sel = jnp.where(hit & (mx > -jnp.inf), 1.0, sel)
        scores = jnp.where(hit, -jnp.inf, scores)
    return sel


def _online_update(s, mask, v, m_ref, l_ref, acc_ref, v_transposed=False):
    s = jnp.where(mask, s, NEG)
    m_old = m_ref[...]
    m_new = jnp.maximum(m_old, jnp.max(s, axis=-1, keepdims=True))
    alpha = jnp.exp(m_old - m_new)
    p = jnp.where(mask, jnp.exp(s - m_new), 0.0)
    l_ref[...] = alpha * l_ref[...] + jnp.sum(p, axis=-1, keepdims=True)
    pv = _dot_nt(p.astype(BF16), v) if v_transposed else _dot(p.astype(BF16), v)
    acc_ref[...] = alpha * acc_ref[...] + pv
    m_ref[...] = m_new


def _reset(m_ref, l_ref, acc_ref):
    m_ref[...] = jnp.full(m_ref.shape, NEG, F32)
    l_ref[...] = jnp.zeros(l_ref.shape, F32)
    acc_ref[...] = jnp.zeros(acc_ref.shape, F32)


def _block_mean_kernel(k_ref, o_ref):
    o_ref[0, 0] = jnp.sum(k_ref[0], axis=0, keepdims=True) * (1.0 / MOBA_BLOCK)


def _block_mean(kv3):
    b, s, _ = kv3.shape
    nb = s // MOBA_BLOCK
    return pl.pallas_call(
        _block_mean_kernel,
        grid=(b, nb),
        in_specs=[pl.BlockSpec((1, MOBA_BLOCK, MIX_WIDTH), lambda i, j: (i, j, 0))],
        out_specs=pl.BlockSpec((1, 1, 1, MIX_WIDTH), lambda i, j: (i, j, 0, 0)),
        out_shape=jax.ShapeDtypeStruct((b, nb, 1, MIX_WIDTH), F32),
        compiler_params=_cparams(("parallel", "parallel")),
        name="moba_block_mean",
    )(kv3)


def _top_k_mask_cols(scores, row, k, height):
    sel = jnp.zeros(scores.shape, F32)
    row = row.astype(F32)
    for _ in range(k):
        mx = jnp.max(scores, axis=0, keepdims=True)
        first = jnp.min(jnp.where(scores == mx, row, float(height)), axis=0, keepdims=True)
        hit = row == first
        sel = jnp.where(hit & (mx > -jnp.inf), 1.0, sel)
        scores = jnp.where(hit, -jnp.inf, scores)
    return sel


def _softmax_stage(s, mask, m_ref, l_ref, p_ref, a_ref, idx):
    s = jnp.where(mask, s, MASKED)
    m_old = m_ref[idx]
    m_new = jnp.maximum(m_old, jnp.max(s, axis=0, keepdims=True))
    alpha = jnp.exp(m_old - m_new)
    p = jnp.exp(s - m_new)
    l_ref[idx] = alpha * l_ref[idx] + jnp.sum(p, axis=0, keepdims=True)
    m_ref[idx] = m_new
    p_ref[idx] = p.astype(BF16)
    a_ref[idx] = alpha


def _pipelined_flash(n_past, qk, smax, pv):
    qk(0, 0)

    def pair(u, carry):
        t0 = 2 * u
        qk(t0 + 1, 1)
        pv(jnp.maximum(t0 - 1, 0), 1)
        smax(t0, 0, False)
        qk(t0 + 2, 0)
        pv(t0, 0)
        smax(t0 + 1, 1, False)
        return carry

    lax.fori_loop(0, n_past // 2, pair, 0)
    odd = n_past % 2 == 1

    @pl.when(odd)
    def _():
        t0 = n_past - 1
        qk(n_past, 1)
        pv(jnp.maximum(t0 - 1, 0), 1)
        smax(t0, 0, False)
        pv(t0, 0)
        smax(n_past, 1, True)
        pv(n_past, 1)

    @pl.when(jnp.logical_not(odd))
    def _():
        pv(jnp.maximum(n_past - 1, 0), 1)
        smax(n_past, 0, True)
        pv(n_past, 0)


def _moba_attn_kernel(qt_ref, km_ref, k_ref, vt_ref, o_ref, sel_ref, m_ref, l_ref, acc_ref,
                      s0_ref, s1_ref, p0_ref, p1_ref, a0_ref, a1_ref, *, hp):
    i = pl.program_id(2)
    blk = MOBA_BLOCK
    nb = km_ref.shape[2]
    s_buf, p_buf, a_buf = (s0_ref, s1_ref), (p0_ref, p1_ref), (a0_ref, a1_ref)
    blk_id = lax.broadcasted_iota(jnp.int32, (nb, blk), 0)
    for hh in range(hp):
        sb = jnp.where(blk_id < i, _dot(km_ref[0, hh], qt_ref[0, hh]), -jnp.inf)
        sel_ref[hh] = _top_k_mask_cols(sb, blk_id, min(MOBA_TOPK, nb), nb)
    m_ref[...] = jnp.full(m_ref.shape, M_FLOOR, F32)
    l_ref[...] = jnp.zeros(l_ref.shape, F32)
    acc_ref[...] = jnp.zeros(acc_ref.shape, F32)
    p1_ref[...] = jnp.zeros(p1_ref.shape, BF16)
    a1_ref[...] = jnp.ones(a1_ref.shape, F32)
    causal = lax.broadcasted_iota(jnp.int32, (blk, blk), 0) <= lax.broadcasted_iota(jnp.int32, (blk, blk), 1)

    def qk(t, slot):
        rows = pl.ds(pl.multiple_of(t * blk, blk), blk)
        for hh in range(hp):
            s_buf[slot][hh] = _dot(k_ref[0, hh, rows, :], qt_ref[0, hh])

    def smax(t, slot, diagonal):
        for hh in range(hp):
            mask = causal if diagonal else sel_ref[hh, pl.ds(t, 1), :] > 0.5
            _softmax_stage(s_buf[slot][hh], mask, m_ref, l_ref, p_buf[slot], a_buf[slot], hh)

    def pv(t, slot):
        for hh in range(hp):
            acc_ref[hh] = a_buf[slot][hh] * acc_ref[hh] + _dot(vt_ref[0, hh, t], p_buf[slot][hh])

    _pipelined_flash(i, qk, smax, pv)
    for hh in range(hp):
        o_ref[0, hh] = acc_ref[hh] / l_ref[hh]


def _moba_attn(qt, km, k, vt, hp=4):
    b, h, d, s = qt.shape
    nb = s // MOBA_BLOCK
    blk = MOBA_BLOCK
    assert h % hp == 0
    per_bh = lambda n: (lambda bi, hi, i: (bi, hi) + (0,) * n)
    return pl.pallas_call(
        functools.partial(_moba_attn_kernel, hp=hp),
        grid=(b, h // hp, nb),
        in_specs=[
            pl.BlockSpec((1, hp, d, blk), lambda bi, hi, i: (bi, hi, 0, i)),
            pl.BlockSpec((1, hp, nb, d), per_bh(2)),
            pl.BlockSpec((1, hp, s, d), per_bh(2)),
            pl.BlockSpec((1, hp, nb, d, blk), per_bh(3)),
        ],
        out_specs=pl.BlockSpec((1, hp, d, blk), lambda bi, hi, i: (bi, hi, 0, i)),
        out_shape=jax.ShapeDtypeStruct((b, h, d, s), F32),
        scratch_shapes=[pltpu.VMEM((hp, nb, blk), F32), pltpu.VMEM((hp, 1, blk), F32), pltpu.VMEM((hp, 1, blk), F32),
                        pltpu.VMEM((hp, d, blk), F32),
                        pltpu.VMEM((hp, blk, blk), F32), pltpu.VMEM((hp, blk, blk), F32),
                        pltpu.VMEM((hp, blk, blk), BF16), pltpu.VMEM((hp, blk, blk), BF16),
                        pltpu.VMEM((hp, 1, blk), F32), pltpu.VMEM((hp, 1, blk), F32)],
        compiler_params=_cparams(("parallel", "parallel", "arbitrary")),
        name="moba_prompt_attn",
    )(qt, km, k, vt)


def _moba_sample_kernel(pt_ref, k0_ref, k1_ref, v0_ref, v1_ref, qbd_ref, kn_ref, vn_ref, o_ref,
                        ob_ref, mb_ref, lb_ref, sc_ref, w_ref, *, n_new):
    del pt_ref
    j = pl.program_id(1)
    nbp = pl.num_programs(1)
    qbd = qbd_ref[0]
    rows = qbd.shape[0]
    s = jnp.concatenate([_dot(qbd, k0_ref[0].astype(BF16)), _dot(qbd, k1_ref[0].astype(BF16))], axis=1)
    sc_ref[j] = jnp.sum(s, axis=-1, keepdims=True) * (1.0 / MOBA_BLOCK)
    m = jnp.max(s, axis=-1, keepdims=True)
    p = jnp.exp(s - m)
    mb_ref[j] = m
    lb_ref[j] = jnp.sum(p, axis=-1, keepdims=True)
    pb = p.astype(BF16)
    ob_ref[j] = (_dot_nt(pb[:, :PAGE_SIZE], v0_ref[0].astype(BF16))
                 + _dot_nt(pb[:, PAGE_SIZE:], v1_ref[0].astype(BF16)))

    @pl.when(j == nbp - 1)
    def _():
        nb = sc_ref.shape[0]
        blk_id = lax.broadcasted_iota(jnp.int32, sc_ref.shape, 0).astype(F32)
        sc = sc_ref[...]
        sel = jnp.zeros(sc.shape, F32)
        for _ in range(MOBA_TOPK):
            mx = jnp.max(sc, axis=0, keepdims=True)
            first = jnp.min(jnp.where(sc == mx, blk_id, float(nb)), axis=0, keepdims=True)
            hit = blk_id == first
            sel = jnp.where(hit, 1.0, sel)
            sc = jnp.where(hit, -jnp.inf, sc)
        s_new = _dot_nt(qbd, kn_ref[0])
        t_row = lax.broadcasted_iota(jnp.int32, s_new.shape, 0) // N_HEADS
        t_col = lax.broadcasted_iota(jnp.int32, s_new.shape, 1)
        ok = (t_col <= t_row) & (t_col < n_new)
        m_new = jnp.max(jnp.where(ok, s_new, NEG), axis=-1, keepdims=True)
        p_new = jnp.where(ok, jnp.exp(s_new - m_new), 0.0)
        l_new = jnp.sum(p_new, axis=-1, keepdims=True)
        o_new = _dot(p_new.astype(BF16), vn_ref[0])
        picked = sel > 0.5
        m_all = jnp.maximum(jnp.max(jnp.where(picked, mb_ref[...], NEG), axis=0), m_new)
        w = jnp.where(picked, jnp.exp(mb_ref[...] - m_all[None]), 0.0)
        w_ref[...] = w
        w_new = jnp.exp(m_new - m_all)
        den = jnp.sum(w * lb_ref[...], axis=0) + w_new * l_new

        def add_block(n, acc):
            return acc + w_ref[n] * ob_ref[n]

        num = lax.fori_loop(0, nb, add_block, w_new * o_new)
        o_ref[0] = num / den


def _moba_sample(page_table, cache_view, qbd, k_new, v_new, n_new):
    db, rows, _ = qbd.shape
    n_pages = page_table.shape[1]
    ppb = MOBA_BLOCK // PAGE_SIZE
    assert ppb == 2 and n_pages % ppb == 0
    nbp = n_pages // ppb
    assert nbp >= MOBA_TOPK
    pt = page_table.reshape(-1)

    def page_spec(u, half):
        return pl.BlockSpec((1, MIX_WIDTH, PAGE_SIZE), lambda b, j, pt_ref: (pt_ref[b * n_pages + ppb * j + u], half, 0))

    per_b = lambda b, j, pt_ref: (b, 0, 0)
    grid_spec = pltpu.PrefetchScalarGridSpec(
        num_scalar_prefetch=1,
        grid=(db, nbp),
        in_specs=[page_spec(0, 0), page_spec(1, 0), page_spec(0, 1), page_spec(1, 1),
                  pl.BlockSpec((1, rows, MIX_WIDTH), per_b),
                  pl.BlockSpec((1, NEW_PAD, MIX_WIDTH), per_b),
                  pl.BlockSpec((1, NEW_PAD, MIX_WIDTH), per_b)],
        out_specs=pl.BlockSpec((1, rows, MIX_WIDTH), per_b),
        scratch_shapes=[pltpu.VMEM((nbp, rows, MIX_WIDTH), F32), pltpu.VMEM((nbp, rows, 1), F32),
                        pltpu.VMEM((nbp, rows, 1), F32), pltpu.VMEM((nbp, rows, 1), F32),
                        pltpu.VMEM((nbp, rows, 1), F32)],
    )
    return pl.pallas_call(
        functools.partial(_moba_sample_kernel, n_new=n_new),
        grid_spec=grid_spec,
        out_shape=jax.ShapeDtypeStruct((db, rows, MIX_WIDTH), F32),
        compiler_params=_cparams(("parallel", "arbitrary")),
        name="moba_sample_attn",
    )(pt, cache_view, cache_view, cache_view, cache_view, qbd, k_new, v_new)


def _compress_kernel(x_ref, pe_ref, w1_ref, b1_ref, w2_ref, kg_ref, o_ref):
    c = pl.program_id(0)
    x = x_ref[0, 0]
    nr = x.shape[0]
    half = x.shape[1]
    pe = pe_ref[0]
    p0 = _dot((x + pe[0:1]).astype(BF16), w1_ref[0, :half, :])
    p1 = _dot((x + pe[1:2]).astype(BF16), w1_ref[0, half:, :])
    h = p0 + pltpu.roll(p1, nr - 1, 0) + b1_ref[0]
    g = 0.5 * h * (1.0 + jnp.tanh(0.7978845608028654 * (h + 0.044715 * (h * h * h))))
    y = _dot(g.astype(BF16), w2_ref[0])
    yn = y * lax.rsqrt(jnp.mean(y * y, axis=-1, keepdims=True) + NORM_EPS) * kg_ref[...]
    y = jnp.where(c == 0, yn, y)
    row = lax.broadcasted_iota(jnp.int32, y.shape, 0)
    o_ref[0, 0] = jnp.where(row < nr - 1, y, 0.0)


def _compress(x16, pe, w1, b1, w2, kg0):
    _, bb, nr, width = x16.shape
    per_c = lambda c, i: (c, 0, 0)
    return pl.pallas_call(
        _compress_kernel,
        grid=(2, bb),
        in_specs=[
            pl.BlockSpec((1, 1, nr, width), lambda c, i: (c, i, 0, 0)),
            pl.BlockSpec((1, 2, width), per_c),
            pl.BlockSpec((1, CMP_LEN * HEAD_DIM, CMP_HIDDEN), per_c),
            pl.BlockSpec((1, 1, CMP_HIDDEN), per_c),
            pl.BlockSpec((1, CMP_HIDDEN, HEAD_DIM), per_c),
            pl.BlockSpec((1, HEAD_DIM), lambda c, i: (0, 0)),
        ],
        out_specs=pl.BlockSpec((1, 1, nr, HEAD_DIM), lambda c, i: (c, i, 0, 0)),
        out_shape=jax.ShapeDtypeStruct((2, bb, nr, HEAD_DIM), F32),
        compiler_params=_cparams(("parallel", "parallel")),
        name="nsa_compress",
    )(x16, pe.reshape(2, 2, width), w1.astype(BF16), b1.reshape(2, 1, CMP_HIDDEN), w2.astype(BF16),
      kg0.reshape(1, HEAD_DIM))


def _page_rows16_kernel(pt_ref, *refs, n_in):
    del pt_ref
    o_ref, t_ref = refs[n_in], refs[n_in + 1]
    rows_per_page = PAGE_SIZE // CMP_STRIDE
    pair = LANES // HEAD_DIM
    for u in range(n_in):
        for comp in range(2):
            for gp in range(NSA_KV_GROUPS // pair):
                slot = (u * 2 + comp) * (NSA_KV_GROUPS // pair) + gp
                r0 = comp * KV_WIDTH + gp * LANES
                t_ref[slot] = refs[u][0, r0:r0 + LANES, :].T
                for c in range(CMP_STRIDE):
                    v = t_ref[slot, pl.ds(c, rows_per_page, stride=CMP_STRIDE), :]
                    for k in range(pair):
                        o_ref[comp, gp * pair + k, u * rows_per_page:(u + 1) * rows_per_page,
                              c * HEAD_DIM:(c + 1) * HEAD_DIM] = v[:, k * HEAD_DIM:(k + 1) * HEAD_DIM]


def _page_rows16(page_table, cache_view, pages_per_step=4):
    db, n_pages = page_table.shape
    assert n_pages % pages_per_step == 0 and PAGE_SIZE == LANES
    rows_per_page = PAGE_SIZE // CMP_STRIDE
    pt = page_table.reshape(-1)

    def page_spec(u):
        return pl.BlockSpec((1, 2 * KV_WIDTH, PAGE_SIZE),
                            lambda b, p, pt_ref: (pt_ref[b * n_pages + p * pages_per_step + u], 0, 0))

    n_slots = pages_per_step * 2 * (NSA_KV_GROUPS * HEAD_DIM // LANES)
    grid_spec = pltpu.PrefetchScalarGridSpec(
        num_scalar_prefetch=1,
        grid=(db, n_pages // pages_per_step),
        in_specs=[page_spec(u) for u in range(pages_per_step)],
        out_specs=pl.BlockSpec((2, NSA_KV_GROUPS, pages_per_step * rows_per_page, CMP_STRIDE * HEAD_DIM),
                               lambda b, p, pt_ref: (0, b, p, 0)),
        scratch_shapes=[pltpu.VMEM((n_slots, PAGE_SIZE, LANES), F32)],
    )
    return pl.pallas_call(
        functools.partial(_page_rows16_kernel, n_in=pages_per_step),
        grid_spec=grid_spec,
        out_shape=jax.ShapeDtypeStruct((2, db * NSA_KV_GROUPS, n_pages * rows_per_page, CMP_STRIDE * HEAD_DIM), F32),
        compiler_params=_cparams(("parallel", "parallel")),
        name="nsa_page_rows16",
    )(pt, *([cache_view] * pages_per_step))


def _overlap(n_rows, n_cmp, n_cols, n_sel):
    i = np.arange(n_rows)[:, None]
    j = np.arange(n_cols)[None, :]
    st = i * CMP_STRIDE
    j0 = j * SEL_BLOCK
    ov = (st < j0 + SEL_BLOCK) & (st + CMP_LEN > j0) & (i < n_cmp) & (j < n_sel)
    return jnp.asarray(ov, BF16)


def _sigmoid(x):
    return 1.0 / (1.0 + jnp.exp(-x))


def _nsa_select(p_sum, ov, pos, n_cols):
    imp = _split_dot(p_sum, ov)
    j = lax.broadcasted_iota(jnp.int32, imp.shape, 1)
    cur = pos // SEL_BLOCK
    forced = (j == 0) | (j == cur) | (j == cur - 1)
    imp = jnp.where(forced, jnp.inf, jnp.where(j <= cur, imp, -jnp.inf))
    return _top_k_mask(imp, j, min(SEL_TOPN, n_cols), n_cols)


def _masked_softmax(s, mask):
    s = jnp.where(mask, s, NEG)
    m = jnp.max(s, axis=-1, keepdims=True)
    p = jnp.where(mask, jnp.exp(s - m), 0.0)
    return p / jnp.maximum(jnp.sum(p, axis=-1, keepdims=True), 1e-30)


def _nsa_attn_kernel(qt_ref, kc_ref, vct_ref, ks_ref, vst_ref, kw_ref, vwt_ref, gt_ref, ovt_ref, o_ref,
                     sel_ref, m_ref, l_ref, acc_ref, s0_ref, s1_ref, p0_ref, p1_ref, a0_ref, a1_ref, *, tq, n_cmp):
    i = pl.program_id(2)
    r = NSA_REP
    qt = qt_ref[0, 0, 0]
    pos = i * tq + lax.broadcasted_iota(jnp.int32, (1, tq), 1)
    key_id = lax.broadcasted_iota(jnp.int32, (tq, tq), 0)
    q_id = lax.broadcasted_iota(jnp.int32, (tq, tq), 1)
    causal = key_id <= q_id
    cols = [slice(rr * tq, (rr + 1) * tq) for rr in range(r)]

    ncp = kc_ref.shape[2]
    n_id = lax.broadcasted_iota(jnp.int32, (ncp, tq), 0)
    cmask = (n_id * CMP_STRIDE + (CMP_LEN - 1) <= pos) & (n_id < n_cmp)
    sc = _dot(kc_ref[0, 0], qt)
    p_sum = jnp.zeros((ncp, tq), F32)
    pcs = []
    for rr in range(r):
        s = jnp.where(cmask, sc[:, cols[rr]], MASKED)
        m = jnp.maximum(jnp.max(s, axis=0, keepdims=True), M_FLOOR)
        p = jnp.exp(s - m)
        pc = p * (1.0 / jnp.maximum(jnp.sum(p, axis=0, keepdims=True), 1e-30))
        p_sum = p_sum + pc
        pcs.append(pc.astype(BF16))
    o_c = _dot(vct_ref[0, 0], jnp.concatenate(pcs, axis=1))

    nsp = ovt_ref.shape[0]
    hi = p_sum.astype(BF16)
    lo = (p_sum - hi.astype(F32)).astype(BF16)
    imp = _dot(ovt_ref[...], hi) + _dot(ovt_ref[...], lo)
    j = lax.broadcasted_iota(jnp.int32, (nsp, tq), 0)
    cur = pos // SEL_BLOCK
    forced = (j == 0) | (j == cur) | (j == cur - 1)
    imp = jnp.where(forced, jnp.inf, jnp.where(j <= cur, imp, -jnp.inf))
    sel_ref[...] = _top_k_mask_cols(imp, j, min(SEL_TOPN, nsp), nsp)

    w_tiles = []
    for back, wmask in ((2, key_id > q_id), (1, None), (0, causal)):
        live = i >= back
        wt = jnp.maximum(i - back, 0)
        mask = live if wmask is None else wmask & live
        w_tiles.append((wt, mask, _dot(kw_ref[0, 0, wt], qt)))
    ow_parts = []
    for rr in range(r):
        ss = [jnp.where(mask, s[:, cols[rr]], MASKED) for _, mask, s in w_tiles]
        m = ss[0].max(axis=0, keepdims=True)
        for s in ss[1:]:
            m = jnp.maximum(m, s.max(axis=0, keepdims=True))
        ps = [jnp.exp(s - m) for s in ss]
        den = sum(p.sum(axis=0, keepdims=True) for p in ps)
        num = sum(_dot(vwt_ref[0, 0, wt], p.astype(BF16)) for (wt, _, _), p in zip(w_tiles, ps))
        ow_parts.append(num / den)

    s_buf, p_buf, a_buf = (s0_ref, s1_ref), (p0_ref, p1_ref), (a0_ref, a1_ref)
    m_ref[...] = jnp.full(m_ref.shape, M_FLOOR, F32)
    l_ref[...] = jnp.zeros(l_ref.shape, F32)
    acc_ref[...] = jnp.zeros(acc_ref.shape, F32)
    p1_ref[...] = jnp.zeros(p1_ref.shape, BF16)
    a1_ref[...] = jnp.ones(a1_ref.shape, F32)
    blocks_per_tile = tq // SEL_BLOCK

    def qk(t, slot):
        s_buf[slot][...] = _dot(ks_ref[0, 0, t], qt)

    def smax(t, slot, diagonal):
        rows = [jnp.broadcast_to(sel_ref[pl.ds(t * blocks_per_tile + u, 1), :], (SEL_BLOCK, tq))
                for u in range(blocks_per_tile)]
        mask = jnp.concatenate(rows, axis=0) > 0.5
        if diagonal:
            mask = mask & causal
        for rr in range(r):
            idx = (slice(None), cols[rr])
            _softmax_stage(s_buf[slot][idx], mask, m_ref, l_ref, p_buf[slot], a_buf[slot], idx)

    def pv(t, slot):
        acc_ref[...] = a_buf[slot][...] * acc_ref[...] + _dot(vst_ref[0, 0, t], p_buf[slot][...])

    _pipelined_flash(i, qk, smax, pv)
    o_s = acc_ref[...] / l_ref[...]

    gates = _sigmoid(gt_ref[0, 0])
    for rr in range(r):
        o_ref[0, 0, rr] = (gates[3 * rr:3 * rr + 1] * o_c[:, cols[rr]] + gates[3 * rr + 1:3 * rr + 2] * o_s[:, cols[rr]]
                           + gates[3 * rr + 2:3 * rr + 3] * ow_parts[rr])


def _nsa_attn(qt, kc, vct, ks, vst, kw, vwt, gt, ovt, n_cmp):
    b, g, nq, d, w = qt.shape
    r = NSA_REP
    tq = w // r
    s = nq * tq
    ncp = kc.shape[2]
    nsp = ovt.shape[0]
    assert ks.shape[2:] == (nq, tq, d) and kw.shape[2:] == (nq, tq, d) and WINDOW == 2 * tq and tq % SEL_BLOCK == 0
    per_bg = lambda n: (lambda bi, gi, i: (bi, gi) + (0,) * n)
    return pl.pallas_call(
        functools.partial(_nsa_attn_kernel, tq=tq, n_cmp=n_cmp),
        grid=(b, g, nq),
        in_specs=[
            pl.BlockSpec((1, 1, 1, d, w), lambda bi, gi, i: (bi, gi, i, 0, 0)),
            pl.BlockSpec((1, 1, ncp, d), per_bg(2)),
            pl.BlockSpec((1, 1, d, ncp), per_bg(2)),
            pl.BlockSpec((1, 1, nq, tq, d), per_bg(3)),
            pl.BlockSpec((1, 1, nq, d, tq), per_bg(3)),
            pl.BlockSpec((1, 1, nq, tq, d), per_bg(3)),
            pl.BlockSpec((1, 1, nq, d, tq), per_bg(3)),
            pl.BlockSpec((1, 1, 16, tq), lambda bi, gi, i: (bi, gi, 0, i)),
            pl.BlockSpec((nsp, ncp), lambda bi, gi, i: (0, 0)),
        ],
        out_specs=pl.BlockSpec((1, 1, r, d, tq), lambda bi, gi, i: (bi, gi, 0, 0, i)),
        out_shape=jax.ShapeDtypeStruct((b, g, r, d, s), F32),
        scratch_shapes=[pltpu.VMEM((nsp, tq), F32), pltpu.VMEM((1, w), F32), pltpu.VMEM((1, w), F32),
                        pltpu.VMEM((d, w), F32),
                        pltpu.VMEM((tq, w), F32), pltpu.VMEM((tq, w), F32),
                        pltpu.VMEM((tq, w), BF16), pltpu.VMEM((tq, w), BF16),
                        pltpu.VMEM((1, w), F32), pltpu.VMEM((1, w), F32)],
        compiler_params=_cparams(("parallel", "parallel", "arbitrary")),
        name="nsa_prompt_attn",
    )(qt, kc, vct, ks, vst, kw, vwt, gt, ovt)


def _nsa_sample_kernel(pt_ref, *refs, n_new, p_len, n_cmp, pps):
    del pt_ref
    pg_refs = refs[:pps]
    (qbd_ref, kc_ref, vc_ref, win_ref, ksn_ref, vsn_ref, kwn_ref, vwn_ref, gt_ref, ov_ref, ex_ref, o_ref,
     sel_ref, oc_ref, m_ref, l_ref, acc_ref) = refs[pps:]
    p = pl.program_id(1)
    n_pg = pl.num_programs(1)
    qbd = qbd_ref[0]
    rows = qbd.shape[0]
    tg = rows // NSA_REP
    t_of_row = (lax.broadcasted_iota(jnp.int32, (rows, 1), 0) % tg) // NSA_KV_GROUPS
    pos = p_len + t_of_row

    @pl.when(p == 0)
    def _():
        ncp = kc_ref.shape[1]
        n_id = lax.broadcasted_iota(jnp.int32, (rows, ncp), 1)
        cmask = (n_id * CMP_STRIDE + (CMP_LEN - 1) <= pos) & (n_id < n_cmp)
        pc = _masked_softmax(_dot_nt(qbd, kc_ref[0]), cmask)
        oc_ref[...] = _dot(pc.astype(BF16), vc_ref[0])
        p_sum = pc[0:tg]
        for rr in range(1, NSA_REP):
            p_sum = p_sum + pc[rr * tg:(rr + 1) * tg]
        sel = _nsa_select(p_sum, ov_ref[...], pos[0:tg], ov_ref.shape[-1])
        sel_ref[...] = jnp.concatenate([sel] * NSA_REP, axis=0)
        _reset(m_ref, l_ref, acc_ref)

    kt = jnp.concatenate([r_[0, :KV_WIDTH, :] for r_ in pg_refs], axis=1).astype(BF16)
    vt = jnp.concatenate([r_[0, KV_WIDTH:, :] for r_ in pg_refs], axis=1).astype(BF16)
    chosen = _dot(sel_ref[...].astype(BF16), ex_ref[...]) > 0.5
    _online_update(_dot(qbd, kt), chosen, vt, m_ref, l_ref, acc_ref, v_transposed=True)

    @pl.when(p == n_pg - 1)
    def _():
        t_col = lax.broadcasted_iota(jnp.int32, (rows, NEW_PAD), 1)
        new_ok = (t_col <= t_of_row) & (t_col < n_new)
        _online_update(_dot_nt(qbd, ksn_ref[0]), new_ok, vsn_ref[0], m_ref, l_ref, acc_ref)
        o_s = acc_ref[...] / l_ref[...]
        _reset(m_ref, l_ref, acc_ref)
        win = win_ref[0]
        wb = win.shape[1]
        kw_pos = p_len - wb + lax.broadcasted_iota(jnp.int32, (rows, wb), 1)
        w_ok = (kw_pos <= pos) & (kw_pos > pos - WINDOW) & (kw_pos >= 0)
        _online_update(_dot(qbd, win[:KV_WIDTH].astype(BF16)), w_ok, win[KV_WIDTH:].astype(BF16),
                       m_ref, l_ref, acc_ref, v_transposed=True)
        _online_update(_dot_nt(qbd, kwn_ref[0]), new_ok, vwn_ref[0], m_ref, l_ref, acc_ref)
        o_w = acc_ref[...] / l_ref[...]
        gates = _sigmoid(gt_ref[0])
        o_ref[0] = gates[:, 0:1] * oc_ref[...] + gates[:, 1:2] * o_s + gates[:, 2:3] * o_w


def _nsa_sample(page_table, cache_view, qbd, kc, vc, win, ks_new, vs_new, kw_new, vw_new, gt, ov, n_new, p_len, n_cmp,
                pages_per_step=4):
    db, rows, _ = qbd.shape
    n_pages = page_table.shape[1]
    ncp = kc.shape[1]
    wb = win.shape[2]
    nsp = ov.shape[-1]
    pps = pages_per_step
    assert n_pages % pps == 0
    pt = page_table.reshape(-1)
    per_b = lambda b, p, pt_ref: (b, 0, 0)
    new_spec = pl.BlockSpec((1, NEW_PAD, KV_WIDTH), per_b)
    tile = pps * PAGE_SIZE
    expand = (jnp.arange(nsp)[:, None] == (jnp.arange(n_pages * PAGE_SIZE) // SEL_BLOCK)[None, :]).astype(BF16)

    def page_spec(u):
        return pl.BlockSpec((1, 2 * KV_WIDTH, PAGE_SIZE), lambda b, p, pt_ref: (pt_ref[b * n_pages + p * pps + u], 1, 0))

    grid_spec = pltpu.PrefetchScalarGridSpec(
        num_scalar_prefetch=1,
        grid=(db, n_pages // pps),
        in_specs=[page_spec(u) for u in range(pps)] + [
            pl.BlockSpec((1, rows, KV_WIDTH), per_b),
            pl.BlockSpec((1, ncp, KV_WIDTH), per_b),
            pl.BlockSpec((1, ncp, KV_WIDTH), per_b),
            pl.BlockSpec((1, 2 * KV_WIDTH, wb), per_b),
            new_spec, new_spec, new_spec, new_spec,
            pl.BlockSpec((1, rows, LANES), per_b),
            pl.BlockSpec((ncp, nsp), lambda b, p, pt_ref: (0, 0)),
            pl.BlockSpec((nsp, tile), lambda b, p, pt_ref: (0, p)),
        ],
        out_specs=pl.BlockSpec((1, rows, KV_WIDTH), per_b),
        scratch_shapes=[pltpu.VMEM((rows, nsp), F32), pltpu.VMEM((rows, KV_WIDTH), F32),
                        pltpu.VMEM((rows, 1), F32), pltpu.VMEM((rows, 1), F32), pltpu.VMEM((rows, KV_WIDTH), F32)],
    )
    return pl.pallas_call(
        functools.partial(_nsa_sample_kernel, n_new=n_new, p_len=p_len, n_cmp=n_cmp, pps=pps),
        grid_spec=grid_spec,
        out_shape=jax.ShapeDtypeStruct((db, rows, KV_WIDTH), F32),
        compiler_params=_cparams(("parallel", "arbitrary")),
        name="nsa_sample_attn",
    )(pt, *([cache_view] * pps), qbd, kc, vc, win, ks_new, vs_new, kw_new, vw_new, gt, ov, expand)


def _row_tile(t_rows):
    return 512 if t_rows % 512 == 0 else t_rows


def _pad_rows(a, n):
    return jnp.pad(a, ((0, 0), (0, n - a.shape[1]), (0, 0)))


def _moba_layer(x_p, x_s, cache, page_table, norm_g, w_in, q_g, k_g, w_out):
    b, s, d = x_p.shape
    db, ds, _ = x_s.shape
    h, hd = N_HEADS, HEAD_DIM
    n_pages = page_table.shape[1]
    p_len = n_pages * PAGE_SIZE
    assert s % MOBA_BLOCK == 0 and p_len % MOBA_BLOCK == 0 and ds <= MOBA_BLOCK
    wq, wkv, wz = w_in[:, :MIX_WIDTH], w_in[:, MIX_WIDTH:3 * MIX_WIDTH], w_in[:, 3 * MIX_WIDTH:]
    gq = jnp.tile(q_g, h)
    gkv = jnp.concatenate([jnp.tile(k_g, h), jnp.ones((MIX_WIDTH,), F32)])
    ones = jnp.ones((MIX_WIDTH,), F32)

    def project(x2d, pos):
        tabs = _rope_tables(pos)
        tm = _row_tile(x2d.shape[0])
        tabs = tabs if pos.shape[0] % tm == 0 else tuple(jnp.tile(t, (tm // pos.shape[0], 1)) for t in tabs)
        q = _proj(x2d, norm_g, wq, [NORM_ROPE] * 4, gq, tabs, tm)
        kv = _proj(x2d, norm_g, wkv, [NORM_ROPE] * 4 + [PLAIN] * 4, gkv, tabs, tm)
        z = _proj(x2d, norm_g, wz, [PLAIN] * 4, ones, tabs, tm)
        return q, kv, z

    xp2 = x_p.reshape(b * s, d)
    q, kv, z = project(xp2, jnp.arange(s, dtype=jnp.int32))
    nb = s // MOBA_BLOCK
    kv3 = kv.reshape(b, s, 2 * MIX_WIDTH)
    km = _block_mean(kv3).reshape(b, nb, h, hd).transpose(0, 2, 1, 3).astype(BF16)
    qt = (q * SCALE).astype(BF16).reshape(b, s, h, hd).transpose(0, 2, 3, 1)
    kv5 = kv.astype(BF16).reshape(b, nb, MOBA_BLOCK, 2, h, hd)
    kk = kv5[:, :, :, 0].transpose(0, 3, 1, 2, 4).reshape(b, h, s, hd)
    vt = kv5[:, :, :, 1].transpose(0, 3, 1, 4, 2)
    o = _moba_attn(qt, km, kk, vt).transpose(0, 3, 1, 2).reshape(b * s, MIX_WIDTH)
    y_p = _out_proj(o, z, xp2, w_out, _row_tile(b * s)).reshape(b, s, d)

    xs2 = x_s.reshape(db * ds, d)
    qs, kvs, zs = project(xs2, jnp.tile(p_len + jnp.arange(ds, dtype=jnp.int32), db))
    rows = ds * h
    q_rows = (qs * SCALE).reshape(db, rows, 1, hd)
    head_of_row = jnp.arange(rows) % h
    diag = (head_of_row[:, None] == jnp.arange(h)[None, :]).astype(F32)
    qbd = (q_rows * diag[None, :, :, None]).astype(BF16).reshape(db, rows, MIX_WIDTH)
    kvs3 = kvs.astype(BF16).reshape(db, ds, 2 * MIX_WIDTH)
    k_new = _pad_rows(kvs3[:, :, :MIX_WIDTH], NEW_PAD)
    v_new = _pad_rows(kvs3[:, :, MIX_WIDTH:], NEW_PAD)
    cache_view = jnp.moveaxis(cache, 1, -1).reshape(cache.shape[0], 2 * MIX_WIDTH, PAGE_SIZE)
    o_full = _moba_sample(page_table, cache_view, qbd, k_new, v_new, ds)
    o6 = o_full.reshape(db, ds, h, h, hd)
    o_s = o6[:, :, jnp.arange(h), jnp.arange(h), :].reshape(db * ds, MIX_WIDTH)
    y_s = _out_proj(o_s, zs, xs2, w_out, db * ds).reshape(db, ds, d)
    return (y_p, y_s, kv.reshape(b, s, 2, h, hd), kvs.reshape(db, ds, 2, h, hd))


def _nsa_layer(x_p, x_s, cache, win_past, page_table, norm_g, w_in, q_g, k_g, cmp_pe, cmp_w1, cmp_b1, cmp_w2, w_out):
    b, s, d = x_p.shape
    db, ds, _ = x_s.shape
    h, hd, g, r = N_HEADS, HEAD_DIM, NSA_KV_GROUPS, NSA_REP
    n_pages = page_table.shape[1]
    p_len = n_pages * PAGE_SIZE
    wb = win_past.shape[1]
    tq = WINDOW // 2
    assert s % tq == 0 and s % CMP_STRIDE == 0 and ds <= SEL_BLOCK and p_len % SEL_BLOCK == 0

    o_q, o_rows, o_win = MIX_WIDTH, MIX_WIDTH + 4 * KV_WIDTH, MIX_WIDTH + 6 * KV_WIDTH
    o_z = o_win + 3 * h
    wq, wrows, wwin = w_in[:, :o_q], w_in[:, o_q:o_rows], w_in[:, o_rows:o_win]
    wgt = jnp.pad(w_in[:, o_win:o_z], ((0, 0), (0, LANES - 3 * h)))
    wz = w_in[:, o_z:]
    ones_kv = jnp.ones((KV_WIDTH,), F32)
    gq = jnp.tile(q_g, h)
    grows = jnp.concatenate([ones_kv, ones_kv, jnp.tile(k_g[1], g), ones_kv])
    gwin = jnp.concatenate([jnp.tile(k_g[2], g), ones_kv])

    def project(x2d, pos):
        tabs = _rope_tables(pos)
        tm = _row_tile(x2d.shape[0])
        tabs = tabs if pos.shape[0] % tm == 0 else tuple(jnp.tile(t, (tm // pos.shape[0], 1)) for t in tabs)
        q = _proj(x2d, norm_g, wq, [NORM_ROPE] * 4, gq, tabs, tm)
        rows_ = _proj(x2d, norm_g, wrows, [ROPE, PLAIN, NORM_ROPE, PLAIN], grows, tabs, tm)
        win_ = _proj(x2d, norm_g, wwin, [NORM_ROPE, PLAIN], gwin, tabs, tm)
        gt_ = _proj(x2d, norm_g, wgt, [PLAIN], jnp.ones((LANES,), F32), tabs, tm, tn=LANES)
        z = _proj(x2d, norm_g, wz, [PLAIN] * 4, jnp.ones((MIX_WIDTH,), F32), tabs, tm)
        return q, rows_, win_, gt_, z

    pe_flat = cmp_pe.reshape(2, 2, CMP_STRIDE * hd)

    xp2 = x_p.reshape(b * s, d)
    q, rows_p, win_p, gt, z = project(xp2, jnp.arange(s, dtype=jnp.int32))
    nr = s // CMP_STRIDE
    n_cmp = (s - CMP_LEN) // CMP_STRIDE + 1
    n_sel = -(-s // SEL_BLOCK)
    nsp = -(-n_sel // LANES) * LANES
    rows6 = rows_p.reshape(b, nr, CMP_STRIDE, 4, g, hd)
    x16 = rows6[:, :, :, :2].transpose(3, 0, 4, 1, 2, 5).reshape(2, b * g, nr, CMP_STRIDE * hd)
    cmp_kv = _compress(x16, pe_flat, cmp_w1, cmp_b1, cmp_w2, k_g[0]).astype(BF16).reshape(2, b, g, nr, hd)
    kc = cmp_kv[0]
    vct = cmp_kv[1].transpose(0, 1, 3, 2)
    nq = s // tq
    rows_bf = rows_p.astype(BF16).reshape(b, nq, tq, 4, g, hd)
    ks = rows_bf[:, :, :, 2].transpose(0, 3, 1, 2, 4)
    vst = rows_bf[:, :, :, 3].transpose(0, 3, 1, 4, 2)
    win_bf = win_p.astype(BF16).reshape(b, nq, tq, 2, g, hd)
    kw = win_bf[:, :, :, 0].transpose(0, 3, 1, 2, 4)
    vwt = win_bf[:, :, :, 1].transpose(0, 3, 1, 4, 2)
    qt = (q * SCALE).astype(BF16).reshape(b, nq, tq, g, r, hd).transpose(0, 3, 1, 5, 4, 2)
    qt = qt.reshape(b, g, nq, hd, r * tq)
    gt_g = gt[:, :3 * h].reshape(b, s, g, 3 * r).transpose(0, 2, 3, 1)
    gt_g = jnp.pad(gt_g, ((0, 0), (0, 0), (0, 16 - 3 * r), (0, 0)))
    ovt = _overlap(nr, n_cmp, nsp, n_sel).T
    o = _nsa_attn(qt, kc, vct, ks, vst, kw, vwt, gt_g, ovt, n_cmp)
    o = o.transpose(0, 4, 1, 2, 3).reshape(b * s, MIX_WIDTH)
    y_p = _out_proj(o, z, xp2, w_out, _row_tile(b * s)).reshape(b, s, d)
    win_len = min(WINDOW, s)
    win_out_p = win_p.reshape(b, s, 2, g, hd)[:, s - win_len:]

    xs2 = x_s.reshape(db * ds, d)
    qs, rows_s, win_s, gts, zs = project(xs2, jnp.tile(p_len + jnp.arange(ds, dtype=jnp.int32), db))
    t_all = p_len + ds
    nr_s = p_len // CMP_STRIDE
    n_cmp_s = (t_all - CMP_LEN) // CMP_STRIDE + 1
    assert n_cmp_s <= nr_s - 1
    n_sel_s = -(-t_all // SEL_BLOCK)
    nsp_s = -(-n_sel_s // LANES) * LANES
    cache_view = jnp.moveaxis(cache, 1, -1).reshape(cache.shape[0], 4 * KV_WIDTH, PAGE_SIZE)
    x16_s = _page_rows16(page_table, cache_view)
    cmp_s = _compress(x16_s, pe_flat, cmp_w1, cmp_b1, cmp_w2, k_g[0]).astype(BF16).reshape(2, db, g, nr_s, hd)
    cmp_s = cmp_s.transpose(0, 1, 3, 2, 4).reshape(2, db, nr_s, KV_WIDTH)
    n_rows = r * ds * g
    q_rows = (qs * SCALE).reshape(db, ds, g, r, hd).transpose(0, 3, 1, 2, 4).reshape(db, n_rows, 1, hd)
    group_of_row = jnp.arange(n_rows) % g
    diag = (group_of_row[:, None] == jnp.arange(g)[None, :]).astype(F32)
    qbd = (q_rows * diag[None, :, :, None]).astype(BF16).reshape(db, n_rows, KV_WIDTH)
    rows_s3 = rows_s.astype(BF16).reshape(db, ds, 4, KV_WIDTH)
    win_s3 = win_s.astype(BF16).reshape(db, ds, 2, KV_WIDTH)
    new = lambda a: _pad_rows(a, NEW_PAD)
    gts_r = gts[:, :3 * h].reshape(db, ds, g, r, 3).transpose(0, 3, 1, 2, 4).reshape(db, n_rows, 3)
    gts_r = jnp.pad(gts_r, ((0, 0), (0, 0), (0, LANES - 3)))
    ov_s = _overlap(nr_s, n_cmp_s, nsp_s, n_sel_s)
    win_view = jnp.moveaxis(win_past, 1, -1).reshape(db, 2 * KV_WIDTH, wb)
    o_full = _nsa_sample(page_table, cache_view, qbd, cmp_s[0], cmp_s[1], win_view,
                         new(rows_s3[:, :, 2]), new(rows_s3[:, :, 3]), new(win_s3[:, :, 0]), new(win_s3[:, :, 1]),
                         gts_r, ov_s, ds, p_len, n_cmp_s)
    o6 = o_full.reshape(db, r, ds, g, g, hd)
    o_s = o6[:, :, :, jnp.arange(g), jnp.arange(g), :]
    o_s = o_s.transpose(0, 2, 3, 1, 4).reshape(db * ds, MIX_WIDTH)
    y_s = _out_proj(o_s, zs, xs2, w_out, db * ds).reshape(db, ds, d)
    win_new = win_s.reshape(db, ds, 2, g, hd)
    win_out_s = jnp.concatenate([win_past, win_new], axis=1)[:, ds:]
    return (y_p, y_s, rows_p.reshape(b, s, 4, g, hd), rows_s.reshape(db, ds, 4, g, hd), win_out_p, win_out_s)


def kernel(x_prompt, x_sample, cache_moba_kv, cache_nsa_kv, state_nsa_win, page_table, a_norm, a_w_in, a_q_norm,
           a_k_norm, a_w_out, b_norm, b_w_in, b_q_norm, b_k_norm, b_cmp_pe, b_cmp_w1, b_cmp_b1, b_cmp_w2, b_w_out):
    depth = a_norm.shape[0] + b_norm.shape[0]
    n_pool = cache_moba_kv.shape[1]
    moba_pool = cache_moba_kv.reshape((-1,) + cache_moba_kv.shape[2:])
    nsa_pool = cache_nsa_kv.reshape((-1,) + cache_nsa_kv.shape[2:])
    xp, xs = x_prompt, x_sample
    moba_p, moba_s, nsa_p, nsa_s, win_p, win_s = [], [], [], [], [], []
    for layer in range(depth):
        i = layer // 2
        pt = page_table + i * n_pool
        if layer % 2 == 0:
            xp, xs, kvp, kvs = _moba_layer(xp, xs, moba_pool, pt, a_norm[i], a_w_in[i], a_q_norm[i],
                                           a_k_norm[i], a_w_out[i])
            moba_p.append(kvp)
            moba_s.append(kvs)
        else:
            xp, xs, rp, rs, wp, ws = _nsa_layer(xp, xs, nsa_pool, state_nsa_win[i], pt, b_norm[i],
                                                b_w_in[i], b_q_norm[i], b_k_norm[i], b_cmp_pe[i], b_cmp_w1[i],
                                                b_cmp_b1[i], b_cmp_w2[i], b_w_out[i])
            nsa_p.append(rp)
            nsa_s.append(rs)
            win_p.append(wp)
            win_s.append(ws)
    return (xp, xs, jnp.stack(moba_p), jnp.stack(moba_s), jnp.stack(nsa_p), jnp.stack(nsa_s),
            jnp.stack(win_p), jnp.stack(win_s))
```

```python
import functools

import numpy as np
import jax
import jax.numpy as jnp
from jax import lax
from jax.experimental import pallas as pl
from jax.experimental.pallas import tpu as pltpu

F32 = jnp.float32
BF16 = jnp.bfloat16

D_MODEL = 1024
N_HEADS = 16
HEAD_DIM = 64
MIX_WIDTH = N_HEADS * HEAD_DIM
ROT_DIM = HEAD_DIM // 4
ROPE_THETA = 500000.0
NORM_EPS = 1e-6
PAGE_SIZE = 128
MOBA_BLOCK = 256
MOBA_TOPK = 3
NSA_KV_GROUPS = 4
NSA_REP = N_HEADS // NSA_KV_GROUPS
KV_WIDTH = NSA_KV_GROUPS * HEAD_DIM
CMP_LEN = 32
CMP_STRIDE = 16
CMP_HIDDEN = 2 * HEAD_DIM
SEL_BLOCK = 64
SEL_TOPN = 16
WINDOW = 512
SCALE = HEAD_DIM ** -0.5

LANES = 128
NEG = -1e30
LOG2E = 1.4426950408889634
BF16_ROWS = 16
AUG = LANES
BIAS_ROWS = BF16_ROWS
V_ROWS = HEAD_DIM + BF16_ROWS
MOBA_HEADS_PER_STEP = 8
M_FLOOR = -1e30
MASKED = -3e38
NEW_PAD = 128
VMEM_LIMIT = 48 * 1024 * 1024

PLAIN, ROPE, NORM_ROPE = 0, 1, 2
NT_DIMS = (((1,), (1,)), ((), ()))


def _cparams(sem):
    return pltpu.CompilerParams(dimension_semantics=sem, vmem_limit_bytes=VMEM_LIMIT)


def _dot(a, b):
    return jnp.dot(a, b, preferred_element_type=F32)


def _dot_nt(a, b):
    return lax.dot_general(a, b, NT_DIMS, preferred_element_type=F32)


def _split_dot(a, b):
    hi = a.astype(BF16)
    lo = (a - hi.astype(F32)).astype(BF16)
    return _dot(hi, b) + _dot(lo, b)


def _proj_kernel(x_ref, g_ref, w_ref, gain_ref, cos_ref, sa_ref, sb_ref, bd_ref, o_ref, xn_ref, *, types, tn):
    j = pl.program_id(1)

    @pl.when(j == 0)
    def _():
        x = x_ref[...]
        ms = jnp.mean(x * x, axis=-1, keepdims=True)
        xn_ref[...] = (x * lax.rsqrt(ms + NORM_EPS) * g_ref[...]).astype(BF16)

    y = _dot(xn_ref[...], w_ref[...])

    def head_norm(v):
        ss = _split_dot(v * v, bd_ref[...])
        return v * lax.rsqrt(ss * (1.0 / HEAD_DIM) + NORM_EPS) * gain_ref[...]

    def rope(v):
        rep = tn // LANES
        c = jnp.concatenate([cos_ref[...]] * rep, axis=1)
        sa = jnp.concatenate([sa_ref[...]] * rep, axis=1)
        sb = jnp.concatenate([sb_ref[...]] * rep, axis=1)
        half = ROT_DIM // 2
        return v * c + pltpu.roll(v, tn - half, 1) * sa + pltpu.roll(v, half, 1) * sb

    def emit(t):
        if t == PLAIN:
            o_ref[...] = y
        elif t == ROPE:
            o_ref[...] = rope(y)
        else:
            o_ref[...] = rope(head_norm(y))

    kinds = sorted(set(types))
    if len(kinds) == 1:
        emit(kinds[0])
    else:
        for t in kinds:
            cond = functools.reduce(jnp.logical_or, [j == jj for jj, tt in enumerate(types) if tt == t])
            pl.when(cond)(functools.partial(emit, t))


def _head_block_diag(tn):
    return jnp.asarray(np.kron(np.eye(tn // HEAD_DIM), np.ones((HEAD_DIM, HEAD_DIM))), BF16)


def _proj(x2d, g, w, types, gain_cols, tabs, tm, tn=256):
    t_rows, d = x2d.shape
    n = w.shape[1]
    assert t_rows % tm == 0 and n % tn == 0 and len(types) == n // tn
    cos, sa, sb = tabs
    assert cos.shape[0] % tm == 0
    ntab = cos.shape[0] // tm
    tab_spec = pl.BlockSpec((tm, LANES), lambda i, j: (i % ntab, 0))
    return pl.pallas_call(
        functools.partial(_proj_kernel, types=tuple(types), tn=tn),
        grid=(t_rows // tm, n // tn),
        in_specs=[
            pl.BlockSpec((tm, d), lambda i, j: (i, 0)),
            pl.BlockSpec((1, d), lambda i, j: (0, 0)),
            pl.BlockSpec((d, tn), lambda i, j: (0, j)),
            pl.BlockSpec((1, tn), lambda i, j: (0, j)),
            tab_spec, tab_spec, tab_spec,
            pl.BlockSpec((tn, tn), lambda i, j: (0, 0)),
        ],
        out_specs=pl.BlockSpec((tm, tn), lambda i, j: (i, j)),
        out_shape=jax.ShapeDtypeStruct((t_rows, n), F32),
        scratch_shapes=[pltpu.VMEM((tm, d), BF16)],
        compiler_params=_cparams(("parallel", "arbitrary")),
        name="rmsnorm_in_proj",
    )(x2d, g.reshape(1, d), w.astype(BF16), gain_cols.reshape(1, n).astype(F32), cos, sa, sb,
      _head_block_diag(tn))


def _rope_tables(pos):
    half = ROT_DIM // 2
    inv_freq = jnp.float32(ROPE_THETA) ** (-jnp.arange(half, dtype=F32) / half)
    ang = pos.astype(F32)[:, None] * inv_freq[None, :]
    cos, sin = jnp.cos(ang), jnp.sin(ang)
    t = pos.shape[0]
    rest = HEAD_DIM - ROT_DIM
    z8, zr, one_r = jnp.zeros((t, half), F32), jnp.zeros((t, rest), F32), jnp.ones((t, rest), F32)
    c = jnp.concatenate([cos, cos, one_r], axis=1)
    sa = jnp.concatenate([-sin, z8, zr], axis=1)
    sb = jnp.concatenate([z8, sin, zr], axis=1)
    rep = LANES // HEAD_DIM
    return tuple(jnp.concatenate([a] * rep, axis=1) for a in (c, sa, sb))


def _out_proj_kernel(o_ref, z_ref, x_ref, w_ref, y_ref):
    z = z_ref[...]
    gated = o_ref[...] * (z * (1.0 / (1.0 + jnp.exp(-z))))
    y_ref[...] = x_ref[...] + _dot(gated.astype(BF16), w_ref[...])


def _out_proj(o2d, z2d, x2d, w_out, tm):
    t_rows, d = x2d.shape
    k = o2d.shape[1]
    row = lambda i: (i, 0)
    return pl.pallas_call(
        _out_proj_kernel,
        grid=(t_rows // tm,),
        in_specs=[pl.BlockSpec((tm, k), row), pl.BlockSpec((tm, k), row), pl.BlockSpec((tm, d), row),
                  pl.BlockSpec((k, d), lambda i: (0, 0))],
        out_specs=pl.BlockSpec((tm, d), row),
        out_shape=jax.ShapeDtypeStruct((t_rows, d), F32),
        compiler_params=_cparams(("parallel",)),
        name="gated_out_proj",
    )(o2d, z2d, x2d, w_out.astype(BF16))


def _top_k_mask(scores, lane, k, width):
    sel = jnp.zeros(scores.shape, F32)
    lane = lane.astype(F32)
    for _ in range(k):
        mx = jnp.max(scores, axis=-1, keepdims=True)
        first = jnp.min(jnp.where(scores == mx, lane, float(width)), axis=-1, keepdims=True)
        hit = lane == first
        sel = jnp.where(hit & (mx > -jnp.inf), 1.0, sel)
        scores = jnp.where(hit, -jnp.inf, scores)
    return sel


def _online_update(s, mask, v, m_ref, l_ref, acc_ref, v_transposed=False):
    s = jnp.where(mask, s, NEG)
    m_old = m_ref[...]
    m_new = jnp.maximum(m_old, jnp.max(s, axis=-1, keepdims=True))
    alpha = jnp.exp(m_old - m_new)
    p = jnp.where(mask, jnp.exp(s - m_new), 0.0)
    l_ref[...] = alpha * l_ref[...] + jnp.sum(p, axis=-1, keepdims=True)
    pv = _dot_nt(p.astype(BF16), v) if v_transposed else _dot(p.astype(BF16), v)
    acc_ref[...] = alpha * acc_ref[...] + pv
    m_ref[...] = m_new


def _reset(m_ref, l_ref, acc_ref):
    m_ref[...] = jnp.full(m_ref.shape, NEG, F32)
    l_ref[...] = jnp.zeros(l_ref.shape, F32)
    acc_ref[...] = jnp.zeros(acc_ref.shape, F32)


def _block_mean_kernel(k_ref, o_ref):
    o_ref[0, 0] = jnp.sum(k_ref[0], axis=0, keepdims=True) * (1.0 / MOBA_BLOCK)


def _block_mean(kv3):
    b, s, _ = kv3.shape
    nb = s // MOBA_BLOCK
    return pl.pallas_call(
        _block_mean_kernel,
        grid=(b, nb),
        in_specs=[pl.BlockSpec((1, MOBA_BLOCK, MIX_WIDTH), lambda i, j: (i, j, 0))],
        out_specs=pl.BlockSpec((1, 1, 1, MIX_WIDTH), lambda i, j: (i, j, 0, 0)),
        out_shape=jax.ShapeDtypeStruct((b, nb, 1, MIX_WIDTH), F32),
        compiler_params=_cparams(("parallel", "parallel")),
        name="moba_block_mean",
    )(kv3)


def _top_k_mask_cols(scores, row, k, height):
    sel = jnp.zeros(scores.shape, F32)
    row = row.astype(F32)
    for _ in range(k):
        mx = jnp.max(scores, axis=0, keepdims=True)
        first = jnp.min(jnp.where(scores == mx, row, float(height)), axis=0, keepdims=True)
        hit = row == first
        sel = jnp.where(hit & (mx > -jnp.inf), 1.0, sel)
        scores = jnp.where(hit, -jnp.inf, scores)
    return sel


def _softmax_stage(s, m_ref, p_ref, a_ref, idx):
    m_old = m_ref[idx]
    m_new = jnp.maximum(m_old, jnp.max(s, axis=0, keepdims=True))
    p_ref[idx] = jnp.exp2(s - m_new).astype(BF16)
    a_ref[idx] = jnp.exp2(m_old - m_new)
    m_ref[idx] = m_new


def _pipelined_flash(n_past, qk, smax, pv):
    qk(0, 0)

    def pair(u, carry):
        t0 = 2 * u
        smax(t0, 0, False)
        qk(t0 + 1, 1)
        pv(jnp.maximum(t0 - 1, 0), 1)
        smax(t0 + 1, 1, False)
        qk(t0 + 2, 0)
        pv(t0, 0)
        return carry

    lax.fori_loop(0, n_past // 2, pair, 0)
    odd = n_past % 2 == 1

    @pl.when(odd)
    def _():
        t0 = n_past - 1
        qk(n_past, 1)
        pv(jnp.maximum(t0 - 1, 0), 1)
        smax(t0, 0, False)
        pv(t0, 0)
        smax(n_past, 1, True)
        pv(n_past, 1)

    @pl.when(jnp.logical_not(odd))
    def _():
        pv(jnp.maximum(n_past - 1, 0), 1)
        smax(n_past, 0, True)
        pv(n_past, 0)


def _bias_rows(keep):
    bias = jnp.where(keep > 0.5, 0.0, MASKED)
    n, width = bias.shape
    if n == 1:
        return jnp.broadcast_to(bias, (BIAS_ROWS, width)).astype(BF16)
    return jnp.concatenate([bias, jnp.zeros((BIAS_ROWS - n, width), F32)], axis=0).astype(BF16)


def _moba_attn_kernel(qt_ref, km_ref, k_ref, vt_ref, o_ref, sel_ref, qa_ref, m_ref, acc_ref,
                      s0_ref, s1_ref, p0_ref, p1_ref, a0_ref, a1_ref, *, hp):
    i = pl.program_id(2)
    blk = MOBA_BLOCK
    nb = km_ref.shape[2]
    s_buf, p_buf, a_buf = (s0_ref, s1_ref), (p0_ref, p1_ref), (a0_ref, a1_ref)
    blk_id = lax.broadcasted_iota(jnp.int32, (nb, blk), 0)
    for hh in range(hp):
        sb = jnp.where(blk_id < i, _dot(km_ref[0, hh], qt_ref[0, hh]), -jnp.inf)
        sel_ref[hh] = _top_k_mask_cols(sb, blk_id, min(MOBA_TOPK, nb), nb)
    qa_ref[...] = qt_ref[0]
    m_ref[...] = jnp.full(m_ref.shape, M_FLOOR, F32)
    acc_ref[...] = jnp.zeros(acc_ref.shape, F32)
    p1_ref[...] = jnp.zeros(p1_ref.shape, BF16)
    a1_ref[...] = jnp.ones(a1_ref.shape, F32)
    causal = lax.broadcasted_iota(jnp.int32, (blk, blk), 0) <= lax.broadcasted_iota(jnp.int32, (blk, blk), 1)

    def qk(t, slot):
        rows = pl.ds(pl.multiple_of(t * blk, blk), blk)
        for hh in range(hp):
            keep = jnp.where(t == i, 1.0, sel_ref[hh, pl.ds(t, 1), :])
            qa_ref[hh, HEAD_DIM:HEAD_DIM + BIAS_ROWS, :] = _bias_rows(keep)
            s_buf[slot][hh] = _dot(k_ref[0, hh, rows, :], qa_ref[hh])

    def smax(t, slot, diagonal):
        for hh in range(hp):
            s = s_buf[slot][hh]
            if diagonal:
                s = jnp.where(causal, s, MASKED)
            _softmax_stage(s, m_ref, p_buf[slot], a_buf[slot], hh)

    def pv(t, slot):
        for hh in range(hp):
            acc_ref[hh] = a_buf[slot][hh] * acc_ref[hh] + _dot(vt_ref[0, hh, t], p_buf[slot][hh])

    _pipelined_flash(i, qk, smax, pv)
    for hh in range(hp):
        o_ref[0, hh] = acc_ref[hh, :HEAD_DIM, :] / acc_ref[hh, HEAD_DIM:HEAD_DIM + 1, :]


def _moba_attn(qt, km, k, vt, hp=MOBA_HEADS_PER_STEP):
    b, h, aug, s = qt.shape
    d = HEAD_DIM
    nb = s // MOBA_BLOCK
    blk = MOBA_BLOCK
    vr = vt.shape[3]
    assert h % hp == 0
    per_bh = lambda n: (lambda bi, hi, i: (bi, hi) + (0,) * n)
    return pl.pallas_call(
        functools.partial(_moba_attn_kernel, hp=hp),
        grid=(b, h // hp, nb),
        in_specs=[
            pl.BlockSpec((1, hp, aug, blk), lambda bi, hi, i: (bi, hi, 0, i)),
            pl.BlockSpec((1, hp, nb, aug), per_bh(2)),
            pl.BlockSpec((1, hp, s, aug), per_bh(2), pipeline_mode=pl.Buffered(1)),
            pl.BlockSpec((1, hp, nb, vr, blk), per_bh(3), pipeline_mode=pl.Buffered(1)),
        ],
        out_specs=pl.BlockSpec((1, hp, d, blk), lambda bi, hi, i: (bi, hi, 0, i)),
        out_shape=jax.ShapeDtypeStruct((b, h, d, s), F32),
        scratch_shapes=[pltpu.VMEM((hp, nb, blk), F32), pltpu.VMEM((hp, aug, blk), BF16), pltpu.VMEM((hp, 1, blk), F32),
                        pltpu.VMEM((hp, vr, blk), F32),
                        pltpu.VMEM((hp, blk, blk), F32), pltpu.VMEM((hp, blk, blk), F32),
                        pltpu.VMEM((hp, blk, blk), BF16), pltpu.VMEM((hp, blk, blk), BF16),
                        pltpu.VMEM((hp, 1, blk), F32), pltpu.VMEM((hp, 1, blk), F32)],
        compiler_params=_cparams(("parallel", "parallel", "arbitrary")),
        name="moba_prompt_attn",
    )(qt, km, k, vt)


def _aug_keys(k, extra):
    pad = AUG - k.shape[-1] - extra.shape[-1]
    return jnp.concatenate([k, extra.astype(k.dtype), jnp.zeros(k.shape[:-1] + (pad,), k.dtype)], axis=-1)


def _aug_values_t(vt):
    shape = vt.shape[:-2]
    keys = vt.shape[-1]
    return jnp.concatenate([vt, jnp.ones(shape + (1, keys), vt.dtype),
                            jnp.zeros(shape + (V_ROWS - vt.shape[-2] - 1, keys), vt.dtype)], axis=-2)


def _pad_to(a, axis, size):
    pad = [(0, 0)] * a.ndim
    pad[axis] = (0, size - a.shape[axis])
    return jnp.pad(a, pad)


def _moba_sample_kernel(pt_ref, *refs, n_new, bps):
    del pt_ref
    ppb = MOBA_BLOCK // PAGE_SIZE
    k_refs, v_refs = refs[:bps * ppb], refs[bps * ppb:2 * bps * ppb]
    qbd_ref, kn_ref, vn_ref, o_ref, ob_ref, mb_ref, lb_ref, sc_ref, w_ref = refs[2 * bps * ppb:]
    j = pl.program_id(1)
    nbp = pl.num_programs(1)
    qbd = qbd_ref[0]
    rows = qbd.shape[0]
    for bb in range(bps):
        blk = j * bps + bb
        s = jnp.concatenate([_dot(qbd, k_refs[bb * ppb + u][0].astype(BF16)) for u in range(ppb)], axis=1)
        sc_ref[blk] = jnp.sum(s, axis=-1, keepdims=True) * (1.0 / MOBA_BLOCK)
        m = jnp.max(s, axis=-1, keepdims=True)
        p = jnp.exp(s - m)
        mb_ref[blk] = m
        lb_ref[blk] = jnp.sum(p, axis=-1, keepdims=True)
        pb = p.astype(BF16)
        ob_ref[blk] = sum(_dot_nt(pb[:, u * PAGE_SIZE:(u + 1) * PAGE_SIZE], v_refs[bb * ppb + u][0].astype(BF16))
                          for u in range(ppb))

    @pl.when(j == nbp - 1)
    def _():
        nb = sc_ref.shape[0]
        blk_id = lax.broadcasted_iota(jnp.int32, sc_ref.shape, 0).astype(F32)
        sc = sc_ref[...]
        sel = jnp.zeros(sc.shape, F32)
        for _ in range(MOBA_TOPK):
            mx = jnp.max(sc, axis=0, keepdims=True)
            first = jnp.min(jnp.where(sc == mx, blk_id, float(nb)), axis=0, keepdims=True)
            hit = blk_id == first
            sel = jnp.where(hit, 1.0, sel)
            sc = jnp.where(hit, -jnp.inf, sc)
        s_new = _dot_nt(qbd, kn_ref[0])
        t_row = lax.broadcasted_iota(jnp.int32, s_new.shape, 0) // N_HEADS
        t_col = lax.broadcasted_iota(jnp.int32, s_new.shape, 1)
        ok = (t_col <= t_row) & (t_col < n_new)
        m_new = jnp.max(jnp.where(ok, s_new, NEG), axis=-1, keepdims=True)
        p_new = jnp.where(ok, jnp.exp(s_new - m_new), 0.0)
        l_new = jnp.sum(p_new, axis=-1, keepdims=True)
        o_new = _dot(p_new.astype(BF16), vn_ref[0])
        picked = sel > 0.5
        m_all = jnp.maximum(jnp.max(jnp.where(picked, mb_ref[...], NEG), axis=0), m_new)
        w = jnp.where(picked, jnp.exp(mb_ref[...] - m_all[None]), 0.0)
        w_ref[...] = w
        w_new = jnp.exp(m_new - m_all)
        den = jnp.sum(w * lb_ref[...], axis=0) + w_new * l_new

        def add_block(n, acc):
            return acc + w_ref[n] * ob_ref[n]

        num = lax.fori_loop(0, nb, add_block, w_new * o_new)
        o_ref[0] = num / den


def _moba_sample(page_table, cache_view, qbd, k_new, v_new, n_new, blocks_per_step=2):
    db, rows, _ = qbd.shape
    n_pages = page_table.shape[1]
    ppb = MOBA_BLOCK // PAGE_SIZE
    bps = blocks_per_step
    assert n_pages % (ppb * bps) == 0
    nbp = n_pages // ppb
    assert nbp >= MOBA_TOPK
    pt = page_table.reshape(-1)
    pages_per_step = ppb * bps

    def page_spec(u, half):
        return pl.BlockSpec((1, MIX_WIDTH, PAGE_SIZE),
                            lambda b, j, pt_ref: (pt_ref[b * n_pages + pages_per_step * j + u], half, 0))

    per_b = lambda b, j, pt_ref: (b, 0, 0)
    grid_spec = pltpu.PrefetchScalarGridSpec(
        num_scalar_prefetch=1,
        grid=(db, nbp // bps),
        in_specs=[page_spec(u, 0) for u in range(pages_per_step)] + [page_spec(u, 1) for u in range(pages_per_step)] + [
                  pl.BlockSpec((1, rows, MIX_WIDTH), per_b),
                  pl.BlockSpec((1, NEW_PAD, MIX_WIDTH), per_b),
                  pl.BlockSpec((1, NEW_PAD, MIX_WIDTH), per_b)],
        out_specs=pl.BlockSpec((1, rows, MIX_WIDTH), per_b),
        scratch_shapes=[pltpu.VMEM((nbp, rows, MIX_WIDTH), F32), pltpu.VMEM((nbp, rows, 1), F32),
                        pltpu.VMEM((nbp, rows, 1), F32), pltpu.VMEM((nbp, rows, 1), F32),
                        pltpu.VMEM((nbp, rows, 1), F32)],
    )
    return pl.pallas_call(
        functools.partial(_moba_sample_kernel, n_new=n_new, bps=bps),
        grid_spec=grid_spec,
        out_shape=jax.ShapeDtypeStruct((db, rows, MIX_WIDTH), F32),
        compiler_params=_cparams(("parallel", "arbitrary")),
        name="moba_sample_attn",
    )(pt, *([cache_view] * (2 * pages_per_step)), qbd, k_new, v_new)


def _compress_kernel(x_ref, pe_ref, w1_ref, b1_ref, w2_ref, kg_ref, o_ref):
    c = pl.program_id(0)
    x = x_ref[0, 0]
    nr = x.shape[0]
    half = x.shape[1]
    pe = pe_ref[0]
    p0 = _dot((x + pe[0:1]).astype(BF16), w1_ref[0, :half, :])
    p1 = _dot((x + pe[1:2]).astype(BF16), w1_ref[0, half:, :])
    h = p0 + pltpu.roll(p1, nr - 1, 0) + b1_ref[0]
    g = 0.5 * h * (1.0 + jnp.tanh(0.7978845608028654 * (h + 0.044715 * (h * h * h))))
    y = _dot(g.astype(BF16), w2_ref[0])
    yn = y * lax.rsqrt(jnp.mean(y * y, axis=-1, keepdims=True) + NORM_EPS) * kg_ref[...]
    y = jnp.where(c == 0, yn, y)
    row = lax.broadcasted_iota(jnp.int32, y.shape, 0)
    o_ref[0, 0] = jnp.where(row < nr - 1, y, 0.0)


def _compress(x16, pe, w1, b1, w2, kg0):
    _, bb, nr, width = x16.shape
    per_c = lambda c, i: (c, 0, 0)
    return pl.pallas_call(
        _compress_kernel,
        grid=(2, bb),
        in_specs=[
            pl.BlockSpec((1, 1, nr, width), lambda c, i: (c, i, 0, 0)),
            pl.BlockSpec((1, 2, width), per_c),
            pl.BlockSpec((1, CMP_LEN * HEAD_DIM, CMP_HIDDEN), per_c),
            pl.BlockSpec((1, 1, CMP_HIDDEN), per_c),
            pl.BlockSpec((1, CMP_HIDDEN, HEAD_DIM), per_c),
            pl.BlockSpec((1, HEAD_DIM), lambda c, i: (0, 0)),
        ],
        out_specs=pl.BlockSpec((1, 1, nr, HEAD_DIM), lambda c, i: (c, i, 0, 0)),
        out_shape=jax.ShapeDtypeStruct((2, bb, nr, HEAD_DIM), F32),
        compiler_params=_cparams(("parallel", "parallel")),
        name="nsa_compress",
    )(x16, pe.reshape(2, 2, width), w1.astype(BF16), b1.reshape(2, 1, CMP_HIDDEN), w2.astype(BF16),
      kg0.reshape(1, HEAD_DIM))


def _page_rows16_kernel(pt_ref, *refs, n_in):
    del pt_ref
    o_ref, t_ref = refs[n_in], refs[n_in + 1]
    rows_per_page = PAGE_SIZE // CMP_STRIDE
    pair = LANES // HEAD_DIM
    for u in range(n_in):
        for comp in range(2):
            for gp in range(NSA_KV_GROUPS // pair):
                slot = (u * 2 + comp) * (NSA_KV_GROUPS // pair) + gp
                r0 = comp * KV_WIDTH + gp * LANES
                t_ref[slot] = refs[u][0, r0:r0 + LANES, :].T
                for c in range(CMP_STRIDE):
                    v = t_ref[slot, pl.ds(c, rows_per_page, stride=CMP_STRIDE), :]
                    for k in range(pair):
                        o_ref[comp, gp * pair + k, u * rows_per_page:(u + 1) * rows_per_page,
                              c * HEAD_DIM:(c + 1) * HEAD_DIM] = v[:, k * HEAD_DIM:(k + 1) * HEAD_DIM]


def _page_rows16(page_table, cache_view, pages_per_step=4):
    db, n_pages = page_table.shape
    assert n_pages % pages_per_step == 0 and PAGE_SIZE == LANES
    rows_per_page = PAGE_SIZE // CMP_STRIDE
    pt = page_table.reshape(-1)

    def page_spec(u):
        return pl.BlockSpec((1, 2 * KV_WIDTH, PAGE_SIZE),
                            lambda b, p, pt_ref: (pt_ref[b * n_pages + p * pages_per_step + u], 0, 0))

    n_slots = pages_per_step * 2 * (NSA_KV_GROUPS * HEAD_DIM // LANES)
    grid_spec = pltpu.PrefetchScalarGridSpec(
        num_scalar_prefetch=1,
        grid=(db, n_pages // pages_per_step),
        in_specs=[page_spec(u) for u in range(pages_per_step)],
        out_specs=pl.BlockSpec((2, NSA_KV_GROUPS, pages_per_step * rows_per_page, CMP_STRIDE * HEAD_DIM),
                               lambda b, p, pt_ref: (0, b, p, 0)),
        scratch_shapes=[pltpu.VMEM((n_slots, PAGE_SIZE, LANES), F32)],
    )
    return pl.pallas_call(
        functools.partial(_page_rows16_kernel, n_in=pages_per_step),
        grid_spec=grid_spec,
        out_shape=jax.ShapeDtypeStruct((2, db * NSA_KV_GROUPS, n_pages * rows_per_page, CMP_STRIDE * HEAD_DIM), F32),
        compiler_params=_cparams(("parallel", "parallel")),
        name="nsa_page_rows16",
    )(pt, *([cache_view] * pages_per_step))


def _overlap(n_rows, n_cmp, n_cols, n_sel):
    i = np.arange(n_rows)[:, None]
    j = np.arange(n_cols)[None, :]
    st = i * CMP_STRIDE
    j0 = j * SEL_BLOCK
    ov = (st < j0 + SEL_BLOCK) & (st + CMP_LEN > j0) & (i < n_cmp) & (j < n_sel)
    return jnp.asarray(ov, BF16)


def _sigmoid(x):
    return 1.0 / (1.0 + jnp.exp(-x))


def _nsa_select(p_sum, ov, pos, n_cols):
    imp = _split_dot(p_sum, ov)
    j = lax.broadcasted_iota(jnp.int32, imp.shape, 1)
    cur = pos // SEL_BLOCK
    forced = (j == 0) | (j == cur) | (j == cur - 1)
    imp = jnp.where(forced, jnp.inf, jnp.where(j <= cur, imp, -jnp.inf))
    return _top_k_mask(imp, j, min(SEL_TOPN, n_cols), n_cols)


def _masked_softmax(s, mask):
    s = jnp.where(mask, s, NEG)
    m = jnp.max(s, axis=-1, keepdims=True)
    p = jnp.where(mask, jnp.exp(s - m), 0.0)
    return p / jnp.maximum(jnp.sum(p, axis=-1, keepdims=True), 1e-30)


def _nsa_attn_kernel(qt_ref, kc_ref, vct_ref, ks_ref, vst_ref, kw_ref, vwt_ref, gt_ref, ovt_ref, o_ref,
                     bias_ref, qa_ref, m_ref, acc_ref, s0_ref, s1_ref, p0_ref, p1_ref, a0_ref, a1_ref,
                     *, tq, n_cmp, gp):
    i = pl.program_id(2)
    r = NSA_REP
    nq = ks_ref.shape[2]
    pos = i * tq + lax.broadcasted_iota(jnp.int32, (1, tq), 1)
    key_id = lax.broadcasted_iota(jnp.int32, (tq, tq), 0)
    q_id = lax.broadcasted_iota(jnp.int32, (tq, tq), 1)
    causal = key_id <= q_id
    cols = [slice(rr * tq, (rr + 1) * tq) for rr in range(r)]
    blocks_per_tile = tq // SEL_BLOCK
    ncp = kc_ref.shape[2]
    nsp = ovt_ref.shape[0]
    n_id = lax.broadcasted_iota(jnp.int32, (ncp, tq), 0)
    cmask = (n_id * CMP_STRIDE + (CMP_LEN - 1) <= pos) & (n_id < n_cmp)
    j = lax.broadcasted_iota(jnp.int32, (nsp, tq), 0)
    cur = pos // SEL_BLOCK
    forced = (j == 0) | (j == cur) | (j == cur - 1)
    for gi in range(gp):
        qt = qt_ref[0, gi, 0]
        sc = _dot(kc_ref[0, gi], qt)
        p_sum = jnp.zeros((ncp, tq), F32)
        pcs = []
        for rr in range(r):
            s = jnp.where(cmask, sc[:, cols[rr]], MASKED)
            m = jnp.maximum(jnp.max(s, axis=0, keepdims=True), M_FLOOR)
            p = jnp.exp2(s - m)
            pc = p * (1.0 / jnp.maximum(jnp.sum(p, axis=0, keepdims=True), 1e-30))
            p_sum = p_sum + pc
            pcs.append(pc.astype(BF16))
        o_c = _dot(vct_ref[0, gi], jnp.concatenate(pcs, axis=1))

        hi = p_sum.astype(BF16)
        lo = (p_sum - hi.astype(F32)).astype(BF16)
        imp = _dot(ovt_ref[...], hi) + _dot(ovt_ref[...], lo)
        imp = jnp.where(forced, jnp.inf, jnp.where(j <= cur, imp, -jnp.inf))
        sel = _top_k_mask_cols(imp, j, min(SEL_TOPN, nsp), nsp)
        for t in range(nq):
            rows = _bias_rows(sel[t * blocks_per_tile:(t + 1) * blocks_per_tile])
            bias_ref[gi, t] = jnp.concatenate([rows] * r, axis=1)

        w_tiles = []
        for back, wmask in ((2, key_id > q_id), (1, None), (0, causal)):
            live = i >= back
            wt = jnp.maximum(i - back, 0)
            mask = live if wmask is None else wmask & live
            w_tiles.append((wt, mask, _dot(kw_ref[0, gi, wt], qt)))
        gates = _sigmoid(gt_ref[0, gi])
        for rr in range(r):
            ss = [jnp.where(mask, s[:, cols[rr]], MASKED) for _, mask, s in w_tiles]
            m = ss[0].max(axis=0, keepdims=True)
            for s in ss[1:]:
                m = jnp.maximum(m, s.max(axis=0, keepdims=True))
            num = sum(_dot(vwt_ref[0, gi, wt], jnp.exp2(s - m).astype(BF16)) for (wt, _, _), s in zip(w_tiles, ss))
            o_w = num[:HEAD_DIM] / num[HEAD_DIM:HEAD_DIM + 1]
            o_ref[0, gi, rr] = gates[3 * rr:3 * rr + 1] * o_c[:, cols[rr]] + gates[3 * rr + 2:3 * rr + 3] * o_w
        qa_ref[gi] = qt

    s_buf, p_buf, a_buf = (s0_ref, s1_ref), (p0_ref, p1_ref), (a0_ref, a1_ref)
    m_ref[...] = jnp.full(m_ref.shape, M_FLOOR, F32)
    acc_ref[...] = jnp.zeros(acc_ref.shape, F32)
    p1_ref[...] = jnp.zeros(p1_ref.shape, BF16)
    a1_ref[...] = jnp.ones(a1_ref.shape, F32)

    def qk(t, slot):
        for gi in range(gp):
            qa_ref[gi, HEAD_DIM:HEAD_DIM + BIAS_ROWS, :] = bias_ref[gi, t]
            s_buf[slot][gi] = _dot(ks_ref[0, gi, t], qa_ref[gi])

    def smax(t, slot, diagonal):
        for gi in range(gp):
            for rr in range(r):
                idx = (gi, slice(None), cols[rr])
                s = s_buf[slot][idx]
                if diagonal:
                    s = jnp.where(causal, s, MASKED)
                _softmax_stage(s, m_ref, p_buf[slot], a_buf[slot], idx)

    def pv(t, slot):
        for gi in range(gp):
            acc_ref[gi] = a_buf[slot][gi] * acc_ref[gi] + _dot(vst_ref[0, gi, t], p_buf[slot][gi])

    _pipelined_flash(i, qk, smax, pv)

    for gi in range(gp):
        o_s = acc_ref[gi, :HEAD_DIM, :] / acc_ref[gi, HEAD_DIM:HEAD_DIM + 1, :]
        gates = _sigmoid(gt_ref[0, gi])
        for rr in range(r):
            o_ref[0, gi, rr] += gates[3 * rr + 1:3 * rr + 2] * o_s[:, cols[rr]]


def _nsa_attn(qt, kc, vct, ks, vst, kw, vwt, gt, ovt, n_cmp, gp=2):
    b, g, nq, aug, w = qt.shape
    r = NSA_REP
    d = HEAD_DIM
    tq = w // r
    s = nq * tq
    ncp = kc.shape[2]
    nsp = ovt.shape[0]
    vr = vst.shape[3]
    assert ks.shape[2:] == (nq, tq, aug) and kw.shape[2:] == (nq, tq, aug) and WINDOW == 2 * tq
    assert tq % SEL_BLOCK == 0 and tq // SEL_BLOCK <= BIAS_ROWS and g % gp == 0 and nsp >= nq * (tq // SEL_BLOCK)
    per_bg = lambda n: (lambda bi, gi, i: (bi, gi) + (0,) * n)
    once = pl.Buffered(1)
    return pl.pallas_call(
        functools.partial(_nsa_attn_kernel, tq=tq, n_cmp=n_cmp, gp=gp),
        grid=(b, g // gp, nq),
        in_specs=[
            pl.BlockSpec((1, gp, 1, aug, w), lambda bi, gi, i: (bi, gi, i, 0, 0)),
            pl.BlockSpec((1, gp, ncp, aug), per_bg(2), pipeline_mode=once),
            pl.BlockSpec((1, gp, d, ncp), per_bg(2), pipeline_mode=once),
            pl.BlockSpec((1, gp, nq, tq, aug), per_bg(3), pipeline_mode=once),
            pl.BlockSpec((1, gp, nq, vr, tq), per_bg(3), pipeline_mode=once),
            pl.BlockSpec((1, gp, nq, tq, aug), per_bg(3), pipeline_mode=once),
            pl.BlockSpec((1, gp, nq, vr, tq), per_bg(3), pipeline_mode=once),
            pl.BlockSpec((1, gp, 16, tq), lambda bi, gi, i: (bi, gi, 0, i)),
            pl.BlockSpec((nsp, ncp), lambda bi, gi, i: (0, 0)),
        ],
        out_specs=pl.BlockSpec((1, gp, r, d, tq), lambda bi, gi, i: (bi, gi, 0, 0, i)),
        out_shape=jax.ShapeDtypeStruct((b, g, r, d, s), F32),
        scratch_shapes=[pltpu.VMEM((gp, nq, BIAS_ROWS, w), BF16), pltpu.VMEM((gp, aug, w), BF16),
                        pltpu.VMEM((gp, 1, w), F32), pltpu.VMEM((gp, vr, w), F32),
                        pltpu.VMEM((gp, tq, w), F32), pltpu.VMEM((gp, tq, w), F32),
                        pltpu.VMEM((gp, tq, w), BF16), pltpu.VMEM((gp, tq, w), BF16),
                        pltpu.VMEM((gp, 1, w), F32), pltpu.VMEM((gp, 1, w), F32)],
        compiler_params=_cparams(("parallel", "parallel", "arbitrary")),
        name="nsa_prompt_attn",
    )(qt, kc, vct, ks, vst, kw, vwt, gt, ovt)


def _nsa_sample_kernel(pt_ref, *refs, n_new, p_len, n_cmp, pps):
    del pt_ref
    pg_refs = refs[:pps]
    (qbd_ref, kc_ref, vc_ref, win_ref, ksn_ref, vsn_ref, kwn_ref, vwn_ref, gt_ref, ov_ref, ex_ref, o_ref,
     sel_ref, oc_ref, m_ref, l_ref, acc_ref) = refs[pps:]
    p = pl.program_id(1)
    n_pg = pl.num_programs(1)
    qbd = qbd_ref[0]
    rows = qbd.shape[0]
    tg = rows // NSA_REP
    t_of_row = (lax.broadcasted_iota(jnp.int32, (rows, 1), 0) % tg) // NSA_KV_GROUPS
    pos = p_len + t_of_row

    @pl.when(p == 0)
    def _():
        ncp = kc_ref.shape[1]
        n_id = lax.broadcasted_iota(jnp.int32, (rows, ncp), 1)
        cmask = (n_id * CMP_STRIDE + (CMP_LEN - 1) <= pos) & (n_id < n_cmp)
        pc = _masked_softmax(_dot_nt(qbd, kc_ref[0]), cmask)
        oc_ref[...] = _dot(pc.astype(BF16), vc_ref[0])
        p_sum = pc[0:tg]
        for rr in range(1, NSA_REP):
            p_sum = p_sum + pc[rr * tg:(rr + 1) * tg]
        sel = _nsa_select(p_sum, ov_ref[...], pos[0:tg], ov_ref.shape[-1])
        sel_ref[...] = jnp.concatenate([sel] * NSA_REP, axis=0)
        _reset(m_ref, l_ref, acc_ref)

    kt = jnp.concatenate([r_[0, :KV_WIDTH, :] for r_ in pg_refs], axis=1).astype(BF16)
    vt = jnp.concatenate([r_[0, KV_WIDTH:, :] for r_ in pg_refs], axis=1).astype(BF16)
    chosen = _dot(sel_ref[...].astype(BF16), ex_ref[...]) > 0.5
    _online_update(_dot(qbd, kt), chosen, vt, m_ref, l_ref, acc_ref, v_transposed=True)

    @pl.when(p == n_pg - 1)
    def _():
        t_col = lax.broadcasted_iota(jnp.int32, (rows, NEW_PAD), 1)
        new_ok = (t_col <= t_of_row) & (t_col < n_new)
        _online_update(_dot_nt(qbd, ksn_ref[0]), new_ok, vsn_ref[0], m_ref, l_ref, acc_ref)
        o_s = acc_ref[...] / l_ref[...]
        _reset(m_ref, l_ref, acc_ref)
        win = win_ref[0]
        wb = win.shape[1]
        kw_pos = p_len - wb + lax.broadcasted_iota(jnp.int32, (rows, wb), 1)
        w_ok = (kw_pos <= pos) & (kw_pos > pos - WINDOW) & (kw_pos >= 0)
        _online_update(_dot(qbd, win[:KV_WIDTH].astype(BF16)), w_ok, win[KV_WIDTH:].astype(BF16),
                       m_ref, l_ref, acc_ref, v_transposed=True)
        _online_update(_dot_nt(qbd, kwn_ref[0]), new_ok, vwn_ref[0], m_ref, l_ref, acc_ref)
        o_w = acc_ref[...] / l_ref[...]
        gates = _sigmoid(gt_ref[0])
        o_ref[0] = gates[:, 0:1] * oc_ref[...] + gates[:, 1:2] * o_s + gates[:, 2:3] * o_w


def _nsa_sample(page_table, cache_view, qbd, kc, vc, win, ks_new, vs_new, kw_new, vw_new, gt, ov, n_new, p_len, n_cmp,
                pages_per_step=4):
    db, rows, _ = qbd.shape
    n_pages = page_table.shape[1]
    ncp = kc.shape[1]
    wb = win.shape[2]
    nsp = ov.shape[-1]
    pps = pages_per_step
    assert n_pages % pps == 0
    pt = page_table.reshape(-1)
    per_b = lambda b, p, pt_ref: (b, 0, 0)
    new_spec = pl.BlockSpec((1, NEW_PAD, KV_WIDTH), per_b)
    tile = pps * PAGE_SIZE
    expand = (jnp.arange(nsp)[:, None] == (jnp.arange(n_pages * PAGE_SIZE) // SEL_BLOCK)[None, :]).astype(BF16)

    def page_spec(u):
        return pl.BlockSpec((1, 2 * KV_WIDTH, PAGE_SIZE), lambda b, p, pt_ref: (pt_ref[b * n_pages + p * pps + u], 1, 0))

    grid_spec = pltpu.PrefetchScalarGridSpec(
        num_scalar_prefetch=1,
        grid=(db, n_pages // pps),
        in_specs=[page_spec(u) for u in range(pps)] + [
            pl.BlockSpec((1, rows, KV_WIDTH), per_b),
            pl.BlockSpec((1, ncp, KV_WIDTH), per_b),
            pl.BlockSpec((1, ncp, KV_WIDTH), per_b),
            pl.BlockSpec((1, 2 * KV_WIDTH, wb), per_b),
            new_spec, new_spec, new_spec, new_spec,
            pl.BlockSpec((1, rows, LANES), per_b),
            pl.BlockSpec((ncp, nsp), lambda b, p, pt_ref: (0, 0)),
            pl.BlockSpec((nsp, tile), lambda b, p, pt_ref: (0, p)),
        ],
        out_specs=pl.BlockSpec((1, rows, KV_WIDTH), per_b),
        scratch_shapes=[pltpu.VMEM((rows, nsp), F32), pltpu.VMEM((rows, KV_WIDTH), F32),
                        pltpu.VMEM((rows, 1), F32), pltpu.VMEM((rows, 1), F32), pltpu.VMEM((rows, KV_WIDTH), F32)],
    )
    return pl.pallas_call(
        functools.partial(_nsa_sample_kernel, n_new=n_new, p_len=p_len, n_cmp=n_cmp, pps=pps),
        grid_spec=grid_spec,
        out_shape=jax.ShapeDtypeStruct((db, rows, KV_WIDTH), F32),
        compiler_params=_cparams(("parallel", "arbitrary")),
        name="nsa_sample_attn",
    )(pt, *([cache_view] * pps), qbd, kc, vc, win, ks_new, vs_new, kw_new, vw_new, gt, ov, expand)


def _row_tile(t_rows):
    return 512 if t_rows % 512 == 0 else t_rows


def _pad_rows(a, n):
    return jnp.pad(a, ((0, 0), (0, n - a.shape[1]), (0, 0)))


def _moba_layer(x_p, x_s, cache, page_table, norm_g, w_in, q_g, k_g, w_out):
    b, s, d = x_p.shape
    db, ds, _ = x_s.shape
    h, hd = N_HEADS, HEAD_DIM
    n_pages = page_table.shape[1]
    p_len = n_pages * PAGE_SIZE
    assert s % MOBA_BLOCK == 0 and p_len % MOBA_BLOCK == 0 and ds <= MOBA_BLOCK
    wq, wkv, wz = w_in[:, :MIX_WIDTH], w_in[:, MIX_WIDTH:3 * MIX_WIDTH], w_in[:, 3 * MIX_WIDTH:]
    gq = jnp.tile(q_g, h)
    gkv = jnp.concatenate([jnp.tile(k_g, h), jnp.ones((MIX_WIDTH,), F32)])
    ones = jnp.ones((MIX_WIDTH,), F32)

    def project(x2d, pos):
        tabs = _rope_tables(pos)
        tm = _row_tile(x2d.shape[0])
        tabs = tabs if pos.shape[0] % tm == 0 else tuple(jnp.tile(t, (tm // pos.shape[0], 1)) for t in tabs)
        q = _proj(x2d, norm_g, wq, [NORM_ROPE] * 4, gq, tabs, tm)
        kv = _proj(x2d, norm_g, wkv, [NORM_ROPE] * 4 + [PLAIN] * 4, gkv, tabs, tm)
        z = _proj(x2d, norm_g, wz, [PLAIN] * 4, ones, tabs, tm)
        return q, kv, z

    xp2 = x_p.reshape(b * s, d)
    q, kv, z = project(xp2, jnp.arange(s, dtype=jnp.int32))
    nb = s // MOBA_BLOCK
    kv3 = kv.reshape(b, s, 2 * MIX_WIDTH)
    km = _block_mean(kv3).reshape(b, nb, h, hd).transpose(0, 2, 1, 3).astype(BF16)
    qt = (q * (SCALE * LOG2E)).astype(BF16).reshape(b, s, h, hd).transpose(0, 2, 3, 1)
    kv5 = kv.astype(BF16).reshape(b, nb, MOBA_BLOCK, 2, h, hd)
    kk = kv5[:, :, :, 0].transpose(0, 3, 1, 2, 4).reshape(b, h, s, hd)
    kk = _aug_keys(kk, jnp.ones((b, h, s, 1), BF16))
    vt = _aug_values_t(kv5[:, :, :, 1].transpose(0, 3, 1, 4, 2))
    o = _moba_attn(_pad_to(qt, 2, AUG), _pad_to(km, 3, AUG), kk, vt).transpose(0, 3, 1, 2).reshape(b * s, MIX_WIDTH)
    y_p = _out_proj(o, z, xp2, w_out, _row_tile(b * s)).reshape(b, s, d)

    xs2 = x_s.reshape(db * ds, d)
    qs, kvs, zs = project(xs2, jnp.tile(p_len + jnp.arange(ds, dtype=jnp.int32), db))
    rows = ds * h
    q_rows = (qs * SCALE).reshape(db, rows, 1, hd)
    head_of_row = jnp.arange(rows) % h
    diag = (head_of_row[:, None] == jnp.arange(h)[None, :]).astype(F32)
    qbd = (q_rows * diag[None, :, :, None]).astype(BF16).reshape(db, rows, MIX_WIDTH)
    kvs3 = kvs.astype(BF16).reshape(db, ds, 2 * MIX_WIDTH)
    k_new = _pad_rows(kvs3[:, :, :MIX_WIDTH], NEW_PAD)
    v_new = _pad_rows(kvs3[:, :, MIX_WIDTH:], NEW_PAD)
    cache_view = jnp.moveaxis(cache, 1, -1).reshape(cache.shape[0], 2 * MIX_WIDTH, PAGE_SIZE)
    o_full = _moba_sample(page_table, cache_view, qbd, k_new, v_new, ds)
    o6 = o_full.reshape(db, ds, h, h, hd)
    o_s = o6[:, :, jnp.arange(h), jnp.arange(h), :].reshape(db * ds, MIX_WIDTH)
    y_s = _out_proj(o_s, zs, xs2, w_out, db * ds).reshape(db, ds, d)
    return (y_p, y_s, kv.reshape(b, s, 2, h, hd), kvs.reshape(db, ds, 2, h, hd))


def _nsa_layer(x_p, x_s, cache, win_past, page_table, norm_g, w_in, q_g, k_g, cmp_pe, cmp_w1, cmp_b1, cmp_w2, w_out):
    b, s, d = x_p.shape
    db, ds, _ = x_s.shape
    h, hd, g, r = N_HEADS, HEAD_DIM, NSA_KV_GROUPS, NSA_REP
    n_pages = page_table.shape[1]
    p_len = n_pages * PAGE_SIZE
    wb = win_past.shape[1]
    tq = WINDOW // 2
    assert s % tq == 0 and s % CMP_STRIDE == 0 and ds <= SEL_BLOCK and p_len % SEL_BLOCK == 0

    o_q, o_rows, o_win = MIX_WIDTH, MIX_WIDTH + 4 * KV_WIDTH, MIX_WIDTH + 6 * KV_WIDTH
    o_z = o_win + 3 * h
    wq, wrows, wwin = w_in[:, :o_q], w_in[:, o_q:o_rows], w_in[:, o_rows:o_win]
    wgt = jnp.pad(w_in[:, o_win:o_z], ((0, 0), (0, LANES - 3 * h)))
    wz = w_in[:, o_z:]
    ones_kv = jnp.ones((KV_WIDTH,), F32)
    gq = jnp.tile(q_g, h)
    grows = jnp.concatenate([ones_kv, ones_kv, jnp.tile(k_g[1], g), ones_kv])
    gwin = jnp.concatenate([jnp.tile(k_g[2], g), ones_kv])

    def project(x2d, pos):
        tabs = _rope_tables(pos)
        tm = _row_tile(x2d.shape[0])
        tabs = tabs if pos.shape[0] % tm == 0 else tuple(jnp.tile(t, (tm // pos.shape[0], 1)) for t in tabs)
        q = _proj(x2d, norm_g, wq, [NORM_ROPE] * 4, gq, tabs, tm)
        rows_ = _proj(x2d, norm_g, wrows, [ROPE, PLAIN, NORM_ROPE, PLAIN], grows, tabs, tm)
        win_ = _proj(x2d, norm_g, wwin, [NORM_ROPE, PLAIN], gwin, tabs, tm)
        gt_ = _proj(x2d, norm_g, wgt, [PLAIN], jnp.ones((LANES,), F32), tabs, tm, tn=LANES)
        z = _proj(x2d, norm_g, wz, [PLAIN] * 4, jnp.ones((MIX_WIDTH,), F32), tabs, tm)
        return q, rows_, win_, gt_, z

    pe_flat = cmp_pe.reshape(2, 2, CMP_STRIDE * hd)

    xp2 = x_p.reshape(b * s, d)
    q, rows_p, win_p, gt, z = project(xp2, jnp.arange(s, dtype=jnp.int32))
    nr = s // CMP_STRIDE
    n_cmp = (s - CMP_LEN) // CMP_STRIDE + 1
    n_sel = -(-s // SEL_BLOCK)
    nsp = -(-n_sel // LANES) * LANES
    rows6 = rows_p.reshape(b, nr, CMP_STRIDE, 4, g, hd)
    x16 = rows6[:, :, :, :2].transpose(3, 0, 4, 1, 2, 5).reshape(2, b * g, nr, CMP_STRIDE * hd)
    cmp_kv = _compress(x16, pe_flat, cmp_w1, cmp_b1, cmp_w2, k_g[0]).astype(BF16).reshape(2, b, g, nr, hd)
    kc = _pad_to(cmp_kv[0], 3, AUG)
    vct = cmp_kv[1].transpose(0, 1, 3, 2)
    nq = s // tq
    rows_bf = rows_p.astype(BF16).reshape(b, nq, tq, 4, g, hd)
    block_in_tile = jax.nn.one_hot(jnp.arange(tq) // SEL_BLOCK, tq // SEL_BLOCK, dtype=BF16)
    ks = _aug_keys(rows_bf[:, :, :, 2].transpose(0, 3, 1, 2, 4),
                   jnp.broadcast_to(block_in_tile, (b, g, nq, tq, tq // SEL_BLOCK)))
    vst = _aug_values_t(rows_bf[:, :, :, 3].transpose(0, 3, 1, 4, 2))
    win_bf = win_p.astype(BF16).reshape(b, nq, tq, 2, g, hd)
    kw = _pad_to(win_bf[:, :, :, 0].transpose(0, 3, 1, 2, 4), 4, AUG)
    vwt = _aug_values_t(win_bf[:, :, :, 1].transpose(0, 3, 1, 4, 2))
    qt = (q * (SCALE * LOG2E)).astype(BF16).reshape(b, nq, tq, g, r, hd).transpose(0, 3, 1, 5, 4, 2)
    qt = _pad_to(qt.reshape(b, g, nq, hd, r * tq), 3, AUG)
    gt_g = gt[:, :3 * h].reshape(b, s, g, 3 * r).transpose(0, 2, 3, 1)
    gt_g = jnp.pad(gt_g, ((0, 0), (0, 0), (0, 16 - 3 * r), (0, 0)))
    ovt = _overlap(nr, n_cmp, nsp, n_sel).T
    o = _nsa_attn(qt, kc, vct, ks, vst, kw, vwt, gt_g, ovt, n_cmp)
    o = o.transpose(0, 4, 1, 2, 3).reshape(b * s, MIX_WIDTH)
    y_p = _out_proj(o, z, xp2, w_out, _row_tile(b * s)).reshape(b, s, d)
    win_len = min(WINDOW, s)
    win_out_p = win_p.reshape(b, s, 2, g, hd)[:, s - win_len:]

    xs2 = x_s.reshape(db * ds, d)
    qs, rows_s, win_s, gts, zs = project(xs2, jnp.tile(p_len + jnp.arange(ds, dtype=jnp.int32), db))
    t_all = p_len + ds
    nr_s = p_len // CMP_STRIDE
    n_cmp_s = (t_all - CMP_LEN) // CMP_STRIDE + 1
    assert n_cmp_s <= nr_s - 1
    n_sel_s = -(-t_all // SEL_BLOCK)
    nsp_s = -(-n_sel_s // LANES) * LANES
    cache_view = jnp.moveaxis(cache, 1, -1).reshape(cache.shape[0], 4 * KV_WIDTH, PAGE_SIZE)
    x16_s = _page_rows16(page_table, cache_view)
    cmp_s = _compress(x16_s, pe_flat, cmp_w1, cmp_b1, cmp_w2, k_g[0]).astype(BF16).reshape(2, db, g, nr_s, hd)
    cmp_s = cmp_s.transpose(0, 1, 3, 2, 4).reshape(2, db, nr_s, KV_WIDTH)
    n_rows = r * ds * g
    q_rows = (qs * SCALE).reshape(db, ds, g, r, hd).transpose(0, 3, 1, 2, 4).reshape(db, n_rows, 1, hd)
    group_of_row = jnp.arange(n_rows) % g
    diag = (group_of_row[:, None] == jnp.arange(g)[None, :]).astype(F32)
    qbd = (q_rows * diag[None, :, :, None]).astype(BF16).reshape(db, n_rows, KV_WIDTH)
    rows_s3 = rows_s.astype(BF16).reshape(db, ds, 4, KV_WIDTH)
    win_s3 = win_s.astype(BF16).reshape(db, ds, 2, KV_WIDTH)
    new = lambda a: _pad_rows(a, NEW_PAD)
    gts_r = gts[:, :3 * h].reshape(db, ds, g, r, 3).transpose(0, 3, 1, 2, 4).reshape(db, n_rows, 3)
    gts_r = jnp.pad(gts_r, ((0, 0), (0, 0), (0, LANES - 3)))
    ov_s = _overlap(nr_s, n_cmp_s, nsp_s, n_sel_s)
    win_view = jnp.moveaxis(win_past, 1, -1).reshape(db, 2 * KV_WIDTH, wb)
    o_full = _nsa_sample(page_table, cache_view, qbd, cmp_s[0], cmp_s[1], win_view,
                         new(rows_s3[:, :, 2]), new(rows_s3[:, :, 3]), new(win_s3[:, :, 0]), new(win_s3[:, :, 1]),
                         gts_r, ov_s, ds, p_len, n_cmp_s)
    o6 = o_full.reshape(db, r, ds, g, g, hd)
    o_s = o6[:, :, :, jnp.arange(g), jnp.arange(g), :]
    o_s = o_s.transpose(0, 2, 3, 1, 4).reshape(db * ds, MIX_WIDTH)
    y_s = _out_proj(o_s, zs, xs2, w_out, db * ds).reshape(db, ds, d)
    win_new = win_s.reshape(db, ds, 2, g, hd)
    win_out_s = jnp.concatenate([win_past, win_new], axis=1)[:, ds:]
    return (y_p, y_s, rows_p.reshape(b, s, 4, g, hd), rows_s.reshape(db, ds, 4, g, hd), win_out_p, win_out_s)


def kernel(x_prompt, x_sample, cache_moba_kv, cache_nsa_kv, state_nsa_win, page_table, a_norm, a_w_in, a_q_norm,
           a_k_norm, a_w_out, b_norm, b_w_in, b_q_norm, b_k_norm, b_cmp_pe, b_cmp_w1, b_cmp_b1, b_cmp_w2, b_w_out):
    depth = a_norm.shape[0] + b_norm.shape[0]
    n_pool = cache_moba_kv.shape[1]
    moba_pool = cache_moba_kv.reshape((-1,) + cache_moba_kv.shape[2:])
    nsa_pool = cache_nsa_kv.reshape((-1,) + cache_nsa_kv.shape[2:])
    xp, xs = x_prompt, x_sample
    moba_p, moba_s, nsa_p, nsa_s, win_p, win_s = [], [], [], [], [], []
    for layer in range(depth):
        i = layer // 2
        pt = page_table + i * n_pool
        if layer % 2 == 0:
            xp, xs, kvp, kvs = _moba_layer(xp, xs, moba_pool, pt, a_norm[i], a_w_in[i], a_q_norm[i],
                                           a_k_norm[i], a_w_out[i])
            moba_p.append(kvp)
            moba_s.append(kvs)
        else:
            xp, xs, rp, rs, wp, ws = _nsa_layer(xp, xs, nsa_pool, state_nsa_win[i], pt, b_norm[i],
                                                b_w_in[i], b_q_norm[i], b_k_norm[i], b_cmp_pe[i], b_cmp_w1[i],
                                                b_cmp_b1[i], b_cmp_w2[i], b_w_out[i])
            nsa_p.append(rp)
            nsa_s.append(rs)
            win_p.append(wp)
            win_s.append(ws)
    return (xp, xs, jnp.stack(moba_p), jnp.stack(moba_s), jnp.stack(nsa_p), jnp.stack(nsa_s),
            jnp.stack(win_p), jnp.stack(win_s))
```

```python
import functools

import numpy as np
import jax
import jax.numpy as jnp
from jax import lax
from jax.experimental import pallas as pl
from jax.experimental.pallas import tpu as pltpu

F32 = jnp.float32
BF16 = jnp.bfloat16

D_MODEL = 1024
N_HEADS = 16
HEAD_DIM = 64
MIX_WIDTH = N_HEADS * HEAD_DIM
ROT_DIM = HEAD_DIM // 4
ROPE_THETA = 500000.0
NORM_EPS = 1e-6
PAGE_SIZE = 128
MOBA_BLOCK = 256
MOBA_TOPK = 3
NSA_KV_GROUPS = 4
NSA_REP = N_HEADS // NSA_KV_GROUPS
KV_WIDTH = NSA_KV_GROUPS * HEAD_DIM
CMP_LEN = 32
CMP_STRIDE = 16
CMP_HIDDEN = 2 * HEAD_DIM
SEL_BLOCK = 64
SEL_TOPN = 16
WINDOW = 512
SCALE = HEAD_DIM ** -0.5

LANES = 128
NEG = -1e30
LOG2E = 1.4426950408889634
BF16_ROWS = 16
AUG = LANES
BIAS_ROWS = BF16_ROWS
V_ROWS = HEAD_DIM + BF16_ROWS
PROJ_ROW_TILE = 1024
MOBA_HEADS_PER_STEP = 8
M_FLOOR = -1e30
MASKED = -3e38
NEW_PAD = 128
VMEM_LIMIT = 48 * 1024 * 1024

PLAIN, ROPE, NORM_ROPE = 0, 1, 2
NT_DIMS = (((1,), (1,)), ((), ()))


def _cparams(sem):
    return pltpu.CompilerParams(dimension_semantics=sem, vmem_limit_bytes=VMEM_LIMIT)


def _dot(a, b):
    return jnp.dot(a, b, preferred_element_type=F32)


def _dot_nt(a, b):
    return lax.dot_general(a, b, NT_DIMS, preferred_element_type=F32)


def _split_dot(a, b):
    hi = a.astype(BF16)
    lo = (a - hi.astype(F32)).astype(BF16)
    return _dot(hi, b) + _dot(lo, b)


def _proj_kernel(x_ref, g_ref, w_ref, gain_ref, cos_ref, sa_ref, sb_ref, bd_ref, o_ref, xn_ref, *, types, tn):
    j = pl.program_id(1)

    @pl.when(j == 0)
    def _():
        x = x_ref[...]
        ms = jnp.mean(x * x, axis=-1, keepdims=True)
        xn_ref[...] = (x * lax.rsqrt(ms + NORM_EPS) * g_ref[...]).astype(BF16)

    y = _dot(xn_ref[...], w_ref[...])

    def head_norm(v):
        ss = _split_dot(v * v, bd_ref[...])
        return v * lax.rsqrt(ss * (1.0 / HEAD_DIM) + NORM_EPS) * gain_ref[...]

    def rope(v):
        rep = tn // LANES
        c = jnp.concatenate([cos_ref[...]] * rep, axis=1)
        sa = jnp.concatenate([sa_ref[...]] * rep, axis=1)
        sb = jnp.concatenate([sb_ref[...]] * rep, axis=1)
        half = ROT_DIM // 2
        return v * c + pltpu.roll(v, tn - half, 1) * sa + pltpu.roll(v, half, 1) * sb

    def emit(t):
        if t == PLAIN:
            o_ref[...] = y
        elif t == ROPE:
            o_ref[...] = rope(y)
        else:
            o_ref[...] = rope(head_norm(y))

    kinds = sorted(set(types))
    if len(kinds) == 1:
        emit(kinds[0])
    else:
        for t in kinds:
            cond = functools.reduce(jnp.logical_or, [j == jj for jj, tt in enumerate(types) if tt == t])
            pl.when(cond)(functools.partial(emit, t))


def _head_block_diag(tn):
    return jnp.asarray(np.kron(np.eye(tn // HEAD_DIM), np.ones((HEAD_DIM, HEAD_DIM))), BF16)


def _proj(x2d, g, w, types, gain_cols, tabs, tm, tn=256):
    t_rows, d = x2d.shape
    n = w.shape[1]
    assert t_rows % tm == 0 and n % tn == 0 and len(types) == n // tn
    cos, sa, sb = tabs
    assert cos.shape[0] % tm == 0
    ntab = cos.shape[0] // tm
    tab_spec = pl.BlockSpec((tm, LANES), lambda i, j: (i % ntab, 0))
    return pl.pallas_call(
        functools.partial(_proj_kernel, types=tuple(types), tn=tn),
        grid=(t_rows // tm, n // tn),
        in_specs=[
            pl.BlockSpec((tm, d), lambda i, j: (i, 0)),
            pl.BlockSpec((1, d), lambda i, j: (0, 0)),
            pl.BlockSpec((d, tn), lambda i, j: (0, j)),
            pl.BlockSpec((1, tn), lambda i, j: (0, j)),
            tab_spec, tab_spec, tab_spec,
            pl.BlockSpec((tn, tn), lambda i, j: (0, 0)),
        ],
        out_specs=pl.BlockSpec((tm, tn), lambda i, j: (i, j)),
        out_shape=jax.ShapeDtypeStruct((t_rows, n), F32),
        scratch_shapes=[pltpu.VMEM((tm, d), BF16)],
        compiler_params=_cparams(("parallel", "arbitrary")),
        name="rmsnorm_in_proj",
    )(x2d, g.reshape(1, d), w.astype(BF16), gain_cols.reshape(1, n).astype(F32), cos, sa, sb,
      _head_block_diag(tn))


def _rope_tables(pos):
    half = ROT_DIM // 2
    inv_freq = jnp.float32(ROPE_THETA) ** (-jnp.arange(half, dtype=F32) / half)
    ang = pos.astype(F32)[:, None] * inv_freq[None, :]
    cos, sin = jnp.cos(ang), jnp.sin(ang)
    t = pos.shape[0]
    rest = HEAD_DIM - ROT_DIM
    z8, zr, one_r = jnp.zeros((t, half), F32), jnp.zeros((t, rest), F32), jnp.ones((t, rest), F32)
    c = jnp.concatenate([cos, cos, one_r], axis=1)
    sa = jnp.concatenate([-sin, z8, zr], axis=1)
    sb = jnp.concatenate([z8, sin, zr], axis=1)
    rep = LANES // HEAD_DIM
    return tuple(jnp.concatenate([a] * rep, axis=1) for a in (c, sa, sb))


def _out_proj_kernel(o_ref, z_ref, x_ref, w_ref, y_ref):
    z = z_ref[...]
    gated = o_ref[...] * (z * (1.0 / (1.0 + jnp.exp(-z))))
    y_ref[...] = x_ref[...] + _dot(gated.astype(BF16), w_ref[...])


def _out_proj(o2d, z2d, x2d, w_out, tm):
    t_rows, d = x2d.shape
    k = o2d.shape[1]
    row = lambda i: (i, 0)
    return pl.pallas_call(
        _out_proj_kernel,
        grid=(t_rows // tm,),
        in_specs=[pl.BlockSpec((tm, k), row), pl.BlockSpec((tm, k), row), pl.BlockSpec((tm, d), row),
                  pl.BlockSpec((k, d), lambda i: (0, 0))],
        out_specs=pl.BlockSpec((tm, d), row),
        out_shape=jax.ShapeDtypeStruct((t_rows, d), F32),
        compiler_params=_cparams(("parallel",)),
        name="gated_out_proj",
    )(o2d, z2d, x2d, w_out.astype(BF16))


def _top_k_mask(scores, lane, k, width):
    sel = jnp.zeros(scores.shape, F32)
    lane = lane.astype(F32)
    for _ in range(k):
        mx = jnp.max(scores, axis=-1, keepdims=True)
        first = jnp.min(jnp.where(scores == mx, lane, float(width)), axis=-1, keepdims=True)
        hit = lane == first
        sel = jnp.where(hit & (mx > -jnp.inf), 1.0, sel)
        scores = jnp.where(hit, -jnp.inf, scores)
    return sel


def _online_update(s, mask, v, m_ref, l_ref, acc_ref, v_transposed=False):
    s = jnp.where(mask, s, NEG)
    m_old = m_ref[...]
    m_new = jnp.maximum(m_old, jnp.max(s, axis=-1, keepdims=True))
    alpha = jnp.exp(m_old - m_new)
    p = jnp.where(mask, jnp.exp(s - m_new), 0.0)
    l_ref[...] = alpha * l_ref[...] + jnp.sum(p, axis=-1, keepdims=True)
    pv = _dot_nt(p.astype(BF16), v) if v_transposed else _dot(p.astype(BF16), v)
    acc_ref[...] = alpha * acc_ref[...] + pv
    m_ref[...] = m_new


def _reset(m_ref, l_ref, acc_ref):
    m_ref[...] = jnp.full(m_ref.shape, NEG, F32)
    l_ref[...] = jnp.zeros(l_ref.shape, F32)
    acc_ref[...] = jnp.zeros(acc_ref.shape, F32)


def _block_mean_kernel(k_ref, o_ref):
    o_ref[0, 0] = jnp.sum(k_ref[0], axis=0, keepdims=True) * (1.0 / MOBA_BLOCK)


def _block_mean(kv3):
    b, s, _ = kv3.shape
    nb = s // MOBA_BLOCK
    return pl.pallas_call(
        _block_mean_kernel,
        grid=(b, nb),
        in_specs=[pl.BlockSpec((1, MOBA_BLOCK, MIX_WIDTH), lambda i, j: (i, j, 0))],
        out_specs=pl.BlockSpec((1, 1, 1, MIX_WIDTH), lambda i, j: (i, j, 0, 0)),
        out_shape=jax.ShapeDtypeStruct((b, nb, 1, MIX_WIDTH), F32),
        compiler_params=_cparams(("parallel", "parallel")),
        name="moba_block_mean",
    )(kv3)


def _top_k_mask_cols(scores, row, k, height):
    sel = jnp.zeros(scores.shape, F32)
    row = row.astype(F32)
    for _ in range(k):
        mx = jnp.max(scores, axis=0, keepdims=True)
        first = jnp.min(jnp.where(scores == mx, row, float(height)), axis=0, keepdims=True)
        hit = row == first
        sel = jnp.where(hit & (mx > -jnp.inf), 1.0, sel)
        scores = jnp.where(hit, -jnp.inf, scores)
    return sel


def _softmax_stage(s, m_ref, p_ref, a_ref, idx):
    m_old = m_ref[idx]
    m_new = jnp.maximum(m_old, jnp.max(s, axis=0, keepdims=True))
    p_ref[idx] = jnp.exp2(s - m_new).astype(BF16)
    a_ref[idx] = jnp.exp2(m_old - m_new)
    m_ref[idx] = m_new


def _pipelined_flash(n_past, qk, smax, pv):
    qk(0, 0)

    def pair(u, carry):
        t0 = 2 * u
        smax(t0, 0, False)
        qk(t0 + 1, 1)
        pv(jnp.maximum(t0 - 1, 0), 1)
        smax(t0 + 1, 1, False)
        qk(t0 + 2, 0)
        pv(t0, 0)
        return carry

    lax.fori_loop(0, n_past // 2, pair, 0)
    odd = n_past % 2 == 1

    @pl.when(odd)
    def _():
        t0 = n_past - 1
        qk(n_past, 1)
        pv(jnp.maximum(t0 - 1, 0), 1)
        smax(t0, 0, False)
        pv(t0, 0)
        smax(n_past, 1, True)
        pv(n_past, 1)

    @pl.when(jnp.logical_not(odd))
    def _():
        pv(jnp.maximum(n_past - 1, 0), 1)
        smax(n_past, 0, True)
        pv(n_past, 0)


def _bias_rows(keep):
    bias = jnp.where(keep > 0.5, 0.0, MASKED)
    n, width = bias.shape
    if n == 1:
        return jnp.broadcast_to(bias, (BIAS_ROWS, width)).astype(BF16)
    return jnp.concatenate([bias, jnp.zeros((BIAS_ROWS - n, width), F32)], axis=0).astype(BF16)


def _moba_attn_kernel(qt_ref, km_ref, k_ref, vt_ref, o_ref, sel_ref, qa_ref, m_ref, acc_ref,
                      s0_ref, s1_ref, p0_ref, p1_ref, a0_ref, a1_ref, *, hp):
    i = pl.program_id(2)
    blk = MOBA_BLOCK
    nb = km_ref.shape[2]
    s_buf, p_buf, a_buf = (s0_ref, s1_ref), (p0_ref, p1_ref), (a0_ref, a1_ref)
    blk_id = lax.broadcasted_iota(jnp.int32, (nb, blk), 0)
    for hh in range(hp):
        sb = jnp.where(blk_id < i, _dot(km_ref[0, hh], qt_ref[0, hh]), -jnp.inf)
        sel_ref[hh] = _top_k_mask_cols(sb, blk_id, min(MOBA_TOPK, nb), nb)
    qa_ref[...] = qt_ref[0]
    m_ref[...] = jnp.full(m_ref.shape, M_FLOOR, F32)
    acc_ref[...] = jnp.zeros(acc_ref.shape, F32)
    p1_ref[...] = jnp.zeros(p1_ref.shape, BF16)
    a1_ref[...] = jnp.ones(a1_ref.shape, F32)
    causal = lax.broadcasted_iota(jnp.int32, (blk, blk), 0) <= lax.broadcasted_iota(jnp.int32, (blk, blk), 1)

    def qk(t, slot):
        rows = pl.ds(pl.multiple_of(t * blk, blk), blk)
        for hh in range(hp):
            keep = jnp.where(t == i, 1.0, sel_ref[hh, pl.ds(t, 1), :])
            qa_ref[hh, HEAD_DIM:HEAD_DIM + BIAS_ROWS, :] = _bias_rows(keep)
            s_buf[slot][hh] = _dot(k_ref[0, hh, rows, :], qa_ref[hh])

    def smax(t, slot, diagonal):
        for hh in range(hp):
            s = s_buf[slot][hh]
            if diagonal:
                s = jnp.where(causal, s, MASKED)
            _softmax_stage(s, m_ref, p_buf[slot], a_buf[slot], hh)

    def pv(t, slot):
        for hh in range(hp):
            acc_ref[hh] = a_buf[slot][hh] * acc_ref[hh] + _dot(vt_ref[0, hh, t], p_buf[slot][hh])

    _pipelined_flash(i, qk, smax, pv)
    for hh in range(hp):
        o_ref[0, hh] = acc_ref[hh, :HEAD_DIM, :] / acc_ref[hh, HEAD_DIM:HEAD_DIM + 1, :]


def _moba_attn(qt, km, k, vt, hp=MOBA_HEADS_PER_STEP):
    b, h, aug, s = qt.shape
    d = HEAD_DIM
    nb = s // MOBA_BLOCK
    blk = MOBA_BLOCK
    vr = vt.shape[3]
    assert h % hp == 0
    per_bh = lambda n: (lambda bi, hi, i: (bi, hi) + (0,) * n)
    return pl.pallas_call(
        functools.partial(_moba_attn_kernel, hp=hp),
        grid=(b, h // hp, nb),
        in_specs=[
            pl.BlockSpec((1, hp, aug, blk), lambda bi, hi, i: (bi, hi, 0, i)),
            pl.BlockSpec((1, hp, nb, aug), per_bh(2)),
            pl.BlockSpec((1, hp, s, aug), per_bh(2), pipeline_mode=pl.Buffered(1)),
            pl.BlockSpec((1, hp, nb, vr, blk), per_bh(3), pipeline_mode=pl.Buffered(1)),
        ],
        out_specs=pl.BlockSpec((1, hp, d, blk), lambda bi, hi, i: (bi, hi, 0, i)),
        out_shape=jax.ShapeDtypeStruct((b, h, d, s), F32),
        scratch_shapes=[pltpu.VMEM((hp, nb, blk), F32), pltpu.VMEM((hp, aug, blk), BF16), pltpu.VMEM((hp, 1, blk), F32),
                        pltpu.VMEM((hp, vr, blk), F32),
                        pltpu.VMEM((hp, blk, blk), F32), pltpu.VMEM((hp, blk, blk), F32),
                        pltpu.VMEM((hp, blk, blk), BF16), pltpu.VMEM((hp, blk, blk), BF16),
                        pltpu.VMEM((hp, 1, blk), F32), pltpu.VMEM((hp, 1, blk), F32)],
        compiler_params=_cparams(("parallel", "parallel", "arbitrary")),
        name="moba_prompt_attn",
    )(qt, km, k, vt)


def _aug_keys(k, extra):
    pad = AUG - k.shape[-1] - extra.shape[-1]
    return jnp.concatenate([k, extra.astype(k.dtype), jnp.zeros(k.shape[:-1] + (pad,), k.dtype)], axis=-1)


def _aug_values_t(vt):
    shape = vt.shape[:-2]
    keys = vt.shape[-1]
    return jnp.concatenate([vt, jnp.ones(shape + (1, keys), vt.dtype),
                            jnp.zeros(shape + (V_ROWS - vt.shape[-2] - 1, keys), vt.dtype)], axis=-2)


def _pad_to(a, axis, size):
    pad = [(0, 0)] * a.ndim
    pad[axis] = (0, size - a.shape[axis])
    return jnp.pad(a, pad)


def _moba_sample_kernel(pt_ref, *refs, n_new, bps):
    del pt_ref
    ppb = MOBA_BLOCK // PAGE_SIZE
    k_refs, v_refs = refs[:bps * ppb], refs[bps * ppb:2 * bps * ppb]
    qbd_ref, kn_ref, vn_ref, o_ref, ob_ref, mb_ref, lb_ref, sc_ref, w_ref = refs[2 * bps * ppb:]
    j = pl.program_id(1)
    nbp = pl.num_programs(1)
    qbd = qbd_ref[0]
    rows = qbd.shape[0]
    for bb in range(bps):
        blk = j * bps + bb
        s = jnp.concatenate([_dot(qbd, k_refs[bb * ppb + u][0].astype(BF16)) for u in range(ppb)], axis=1)
        sc_ref[blk] = jnp.sum(s, axis=-1, keepdims=True) * (1.0 / MOBA_BLOCK)
        m = jnp.max(s, axis=-1, keepdims=True)
        p = jnp.exp(s - m)
        mb_ref[blk] = m
        lb_ref[blk] = jnp.sum(p, axis=-1, keepdims=True)
        pb = p.astype(BF16)
        ob_ref[blk] = sum(_dot_nt(pb[:, u * PAGE_SIZE:(u + 1) * PAGE_SIZE], v_refs[bb * ppb + u][0].astype(BF16))
                          for u in range(ppb))

    @pl.when(j == nbp - 1)
    def _():
        nb = sc_ref.shape[0]
        blk_id = lax.broadcasted_iota(jnp.int32, sc_ref.shape, 0).astype(F32)
        sc = sc_ref[...]
        sel = jnp.zeros(sc.shape, F32)
        for _ in range(MOBA_TOPK):
            mx = jnp.max(sc, axis=0, keepdims=True)
            first = jnp.min(jnp.where(sc == mx, blk_id, float(nb)), axis=0, keepdims=True)
            hit = blk_id == first
            sel = jnp.where(hit, 1.0, sel)
            sc = jnp.where(hit, -jnp.inf, sc)
        s_new = _dot_nt(qbd, kn_ref[0])
        t_row = lax.broadcasted_iota(jnp.int32, s_new.shape, 0) // N_HEADS
        t_col = lax.broadcasted_iota(jnp.int32, s_new.shape, 1)
        ok = (t_col <= t_row) & (t_col < n_new)
        m_new = jnp.max(jnp.where(ok, s_new, NEG), axis=-1, keepdims=True)
        p_new = jnp.where(ok, jnp.exp(s_new - m_new), 0.0)
        l_new = jnp.sum(p_new, axis=-1, keepdims=True)
        o_new = _dot(p_new.astype(BF16), vn_ref[0])
        picked = sel > 0.5
        m_all = jnp.maximum(jnp.max(jnp.where(picked, mb_ref[...], NEG), axis=0), m_new)
        w = jnp.where(picked, jnp.exp(mb_ref[...] - m_all[None]), 0.0)
        w_ref[...] = w
        w_new = jnp.exp(m_new - m_all)
        den = jnp.sum(w * lb_ref[...], axis=0) + w_new * l_new

        def add_block(n, acc):
            return acc + w_ref[n] * ob_ref[n]

        num = lax.fori_loop(0, nb, add_block, w_new * o_new)
        o_ref[0] = num / den


def _moba_sample(page_table, cache_view, qbd, k_new, v_new, n_new, blocks_per_step=4):
    db, rows, _ = qbd.shape
    n_pages = page_table.shape[1]
    ppb = MOBA_BLOCK // PAGE_SIZE
    bps = blocks_per_step
    assert n_pages % (ppb * bps) == 0
    nbp = n_pages // ppb
    assert nbp >= MOBA_TOPK
    pt = page_table.reshape(-1)
    pages_per_step = ppb * bps

    def page_spec(u, half):
        return pl.BlockSpec((1, MIX_WIDTH, PAGE_SIZE),
                            lambda b, j, pt_ref: (pt_ref[b * n_pages + pages_per_step * j + u], half, 0))

    per_b = lambda b, j, pt_ref: (b, 0, 0)
    grid_spec = pltpu.PrefetchScalarGridSpec(
        num_scalar_prefetch=1,
        grid=(db, nbp // bps),
        in_specs=[page_spec(u, 0) for u in range(pages_per_step)] + [page_spec(u, 1) for u in range(pages_per_step)] + [
                  pl.BlockSpec((1, rows, MIX_WIDTH), per_b),
                  pl.BlockSpec((1, NEW_PAD, MIX_WIDTH), per_b),
                  pl.BlockSpec((1, NEW_PAD, MIX_WIDTH), per_b)],
        out_specs=pl.BlockSpec((1, rows, MIX_WIDTH), per_b),
        scratch_shapes=[pltpu.VMEM((nbp, rows, MIX_WIDTH), F32), pltpu.VMEM((nbp, rows, 1), F32),
                        pltpu.VMEM((nbp, rows, 1), F32), pltpu.VMEM((nbp, rows, 1), F32),
                        pltpu.VMEM((nbp, rows, 1), F32)],
    )
    return pl.pallas_call(
        functools.partial(_moba_sample_kernel, n_new=n_new, bps=bps),
        grid_spec=grid_spec,
        out_shape=jax.ShapeDtypeStruct((db, rows, MIX_WIDTH), F32),
        compiler_params=_cparams(("parallel", "arbitrary")),
        name="moba_sample_attn",
    )(pt, *([cache_view] * (2 * pages_per_step)), qbd, k_new, v_new)


def _compress_kernel(x_ref, pe_ref, w1_ref, b1_ref, w2_ref, kg_ref, o_ref):
    c = pl.program_id(0)
    x = x_ref[0, 0]
    nr = x.shape[0]
    half = x.shape[1]
    pe = pe_ref[0]
    p0 = _dot((x + pe[0:1]).astype(BF16), w1_ref[0, :half, :])
    p1 = _dot((x + pe[1:2]).astype(BF16), w1_ref[0, half:, :])
    h = p0 + pltpu.roll(p1, nr - 1, 0) + b1_ref[0]
    g = 0.5 * h * (1.0 + jnp.tanh(0.7978845608028654 * (h + 0.044715 * (h * h * h))))
    y = _dot(g.astype(BF16), w2_ref[0])
    yn = y * lax.rsqrt(jnp.mean(y * y, axis=-1, keepdims=True) + NORM_EPS) * kg_ref[...]
    y = jnp.where(c == 0, yn, y)
    row = lax.broadcasted_iota(jnp.int32, y.shape, 0)
    o_ref[0, 0] = jnp.where(row < nr - 1, y, 0.0)


def _compress(x16, pe, w1, b1, w2, kg0):
    _, bb, nr, width = x16.shape
    per_c = lambda c, i: (c, 0, 0)
    return pl.pallas_call(
        _compress_kernel,
        grid=(2, bb),
        in_specs=[
            pl.BlockSpec((1, 1, nr, width), lambda c, i: (c, i, 0, 0)),
            pl.BlockSpec((1, 2, width), per_c),
            pl.BlockSpec((1, CMP_LEN * HEAD_DIM, CMP_HIDDEN), per_c),
            pl.BlockSpec((1, 1, CMP_HIDDEN), per_c),
            pl.BlockSpec((1, CMP_HIDDEN, HEAD_DIM), per_c),
            pl.BlockSpec((1, HEAD_DIM), lambda c, i: (0, 0)),
        ],
        out_specs=pl.BlockSpec((1, 1, nr, HEAD_DIM), lambda c, i: (c, i, 0, 0)),
        out_shape=jax.ShapeDtypeStruct((2, bb, nr, HEAD_DIM), F32),
        compiler_params=_cparams(("parallel", "parallel")),
        name="nsa_compress",
    )(x16, pe.reshape(2, 2, width), w1.astype(BF16), b1.reshape(2, 1, CMP_HIDDEN), w2.astype(BF16),
      kg0.reshape(1, HEAD_DIM))


def _page_rows16_kernel(pt_ref, *refs, n_in):
    del pt_ref
    o_ref, t_ref = refs[n_in], refs[n_in + 1]
    rows_per_page = PAGE_SIZE // CMP_STRIDE
    pair = LANES // HEAD_DIM
    for u in range(n_in):
        for comp in range(2):
            for gp in range(NSA_KV_GROUPS // pair):
                slot = (u * 2 + comp) * (NSA_KV_GROUPS // pair) + gp
                r0 = comp * KV_WIDTH + gp * LANES
                t_ref[slot] = refs[u][0, r0:r0 + LANES, :].T
                for c in range(CMP_STRIDE):
                    v = t_ref[slot, pl.ds(c, rows_per_page, stride=CMP_STRIDE), :]
                    for k in range(pair):
                        o_ref[comp, gp * pair + k, u * rows_per_page:(u + 1) * rows_per_page,
                              c * HEAD_DIM:(c + 1) * HEAD_DIM] = v[:, k * HEAD_DIM:(k + 1) * HEAD_DIM]


def _page_rows16(page_table, cache_view, pages_per_step=8):
    db, n_pages = page_table.shape
    assert n_pages % pages_per_step == 0 and PAGE_SIZE == LANES
    rows_per_page = PAGE_SIZE // CMP_STRIDE
    pt = page_table.reshape(-1)

    def page_spec(u):
        return pl.BlockSpec((1, 2 * KV_WIDTH, PAGE_SIZE),
                            lambda b, p, pt_ref: (pt_ref[b * n_pages + p * pages_per_step + u], 0, 0))

    n_slots = pages_per_step * 2 * (NSA_KV_GROUPS * HEAD_DIM // LANES)
    grid_spec = pltpu.PrefetchScalarGridSpec(
        num_scalar_prefetch=1,
        grid=(db, n_pages // pages_per_step),
        in_specs=[page_spec(u) for u in range(pages_per_step)],
        out_specs=pl.BlockSpec((2, NSA_KV_GROUPS, pages_per_step * rows_per_page, CMP_STRIDE * HEAD_DIM),
                               lambda b, p, pt_ref: (0, b, p, 0)),
        scratch_shapes=[pltpu.VMEM((n_slots, PAGE_SIZE, LANES), F32)],
    )
    return pl.pallas_call(
        functools.partial(_page_rows16_kernel, n_in=pages_per_step),
        grid_spec=grid_spec,
        out_shape=jax.ShapeDtypeStruct((2, db * NSA_KV_GROUPS, n_pages * rows_per_page, CMP_STRIDE * HEAD_DIM), F32),
        compiler_params=_cparams(("parallel", "parallel")),
        name="nsa_page_rows16",
    )(pt, *([cache_view] * pages_per_step))


def _overlap(n_rows, n_cmp, n_cols, n_sel):
    i = np.arange(n_rows)[:, None]
    j = np.arange(n_cols)[None, :]
    st = i * CMP_STRIDE
    j0 = j * SEL_BLOCK
    ov = (st < j0 + SEL_BLOCK) & (st + CMP_LEN > j0) & (i < n_cmp) & (j < n_sel)
    return jnp.asarray(ov, BF16)


def _sigmoid(x):
    return 1.0 / (1.0 + jnp.exp(-x))


def _nsa_select(p_sum, ov, pos, n_cols):
    imp = _split_dot(p_sum, ov)
    j = lax.broadcasted_iota(jnp.int32, imp.shape, 1)
    cur = pos // SEL_BLOCK
    forced = (j == 0) | (j == cur) | (j == cur - 1)
    imp = jnp.where(forced, jnp.inf, jnp.where(j <= cur, imp, -jnp.inf))
    return _top_k_mask(imp, j, min(SEL_TOPN, n_cols), n_cols)


def _masked_softmax(s, mask):
    s = jnp.where(mask, s, NEG)
    m = jnp.max(s, axis=-1, keepdims=True)
    p = jnp.where(mask, jnp.exp(s - m), 0.0)
    return p / jnp.maximum(jnp.sum(p, axis=-1, keepdims=True), 1e-30)


def _nsa_attn_kernel(qt_ref, kc_ref, vct_ref, ks_ref, vst_ref, kw_ref, vwt_ref, gt_ref, ovt_ref, o_ref,
                     bias_ref, qa_ref, m_ref, acc_ref, s0_ref, s1_ref, p0_ref, p1_ref, a0_ref, a1_ref,
                     *, tq, n_cmp, gp):
    i = pl.program_id(2)
    r = NSA_REP
    nq = ks_ref.shape[2]
    pos = i * tq + lax.broadcasted_iota(jnp.int32, (1, tq), 1)
    key_id = lax.broadcasted_iota(jnp.int32, (tq, tq), 0)
    q_id = lax.broadcasted_iota(jnp.int32, (tq, tq), 1)
    causal = key_id <= q_id
    cols = [slice(rr * tq, (rr + 1) * tq) for rr in range(r)]
    blocks_per_tile = tq // SEL_BLOCK
    ncp = kc_ref.shape[2]
    nsp = ovt_ref.shape[0]
    n_id = lax.broadcasted_iota(jnp.int32, (ncp, tq), 0)
    cmask = (n_id * CMP_STRIDE + (CMP_LEN - 1) <= pos) & (n_id < n_cmp)
    j = lax.broadcasted_iota(jnp.int32, (nsp, tq), 0)
    cur = pos // SEL_BLOCK
    forced = (j == 0) | (j == cur) | (j == cur - 1)
    for gi in range(gp):
        qt = qt_ref[0, gi, 0]
        sc = _dot(kc_ref[0, gi], qt)
        p_sum = jnp.zeros((ncp, tq), F32)
        pcs = []
        for rr in range(r):
            s = jnp.where(cmask, sc[:, cols[rr]], MASKED)
            m = jnp.maximum(jnp.max(s, axis=0, keepdims=True), M_FLOOR)
            p = jnp.exp2(s - m)
            pc = p * (1.0 / jnp.maximum(jnp.sum(p, axis=0, keepdims=True), 1e-30))
            p_sum = p_sum + pc
            pcs.append(pc.astype(BF16))
        o_c = _dot(vct_ref[0, gi], jnp.concatenate(pcs, axis=1))

        hi = p_sum.astype(BF16)
        lo = (p_sum - hi.astype(F32)).astype(BF16)
        imp = _dot(ovt_ref[...], hi) + _dot(ovt_ref[...], lo)
        imp = jnp.where(forced, jnp.inf, jnp.where(j <= cur, imp, -jnp.inf))
        sel = _top_k_mask_cols(imp, j, min(SEL_TOPN, nsp), nsp)
        for t in range(nq):
            rows = _bias_rows(sel[t * blocks_per_tile:(t + 1) * blocks_per_tile])
            bias_ref[gi, t] = jnp.concatenate([rows] * r, axis=1)

        w_tiles = []
        for back, wmask in ((2, key_id > q_id), (1, None), (0, causal)):
            live = i >= back
            wt = jnp.maximum(i - back, 0)
            mask = live if wmask is None else wmask & live
            w_tiles.append((wt, mask, _dot(kw_ref[0, gi, wt], qt)))
        gates = _sigmoid(gt_ref[0, gi])
        for rr in range(r):
            ss = [jnp.where(mask, s[:, cols[rr]], MASKED) for _, mask, s in w_tiles]
            m = ss[0].max(axis=0, keepdims=True)
            for s in ss[1:]:
                m = jnp.maximum(m, s.max(axis=0, keepdims=True))
            num = sum(_dot(vwt_ref[0, gi, wt], jnp.exp2(s - m).astype(BF16)) for (wt, _, _), s in zip(w_tiles, ss))
            o_w = num[:HEAD_DIM] / num[HEAD_DIM:HEAD_DIM + 1]
            o_ref[0, gi, rr] = gates[3 * rr:3 * rr + 1] * o_c[:, cols[rr]] + gates[3 * rr + 2:3 * rr + 3] * o_w
        qa_ref[gi] = qt

    s_buf, p_buf, a_buf = (s0_ref, s1_ref), (p0_ref, p1_ref), (a0_ref, a1_ref)
    m_ref[...] = jnp.full(m_ref.shape, M_FLOOR, F32)
    acc_ref[...] = jnp.zeros(acc_ref.shape, F32)
    p1_ref[...] = jnp.zeros(p1_ref.shape, BF16)
    a1_ref[...] = jnp.ones(a1_ref.shape, F32)

    def qk(t, slot):
        for gi in range(gp):
            qa_ref[gi, HEAD_DIM:HEAD_DIM + BIAS_ROWS, :] = bias_ref[gi, t]
            s_buf[slot][gi] = _dot(ks_ref[0, gi, t], qa_ref[gi])

    def smax(t, slot, diagonal):
        for gi in range(gp):
            for rr in range(r):
                idx = (gi, slice(None), cols[rr])
                s = s_buf[slot][idx]
                if diagonal:
                    s = jnp.where(causal, s, MASKED)
                _softmax_stage(s, m_ref, p_buf[slot], a_buf[slot], idx)

    def pv(t, slot):
        for gi in range(gp):
            acc_ref[gi] = a_buf[slot][gi] * acc_ref[gi] + _dot(vst_ref[0, gi, t], p_buf[slot][gi])

    _pipelined_flash(i, qk, smax, pv)

    for gi in range(gp):
        o_s = acc_ref[gi, :HEAD_DIM, :] / acc_ref[gi, HEAD_DIM:HEAD_DIM + 1, :]
        gates = _sigmoid(gt_ref[0, gi])
        for rr in range(r):
            o_ref[0, gi, rr] += gates[3 * rr + 1:3 * rr + 2] * o_s[:, cols[rr]]


def _nsa_attn(qt, kc, vct, ks, vst, kw, vwt, gt, ovt, n_cmp, gp=2):
    b, g, nq, aug, w = qt.shape
    r = NSA_REP
    d = HEAD_DIM
    tq = w // r
    s = nq * tq
    ncp = kc.shape[2]
    nsp = ovt.shape[0]
    vr = vst.shape[3]
    assert ks.shape[2:] == (nq, tq, aug) and kw.shape[2:] == (nq, tq, aug) and WINDOW == 2 * tq
    assert tq % SEL_BLOCK == 0 and tq // SEL_BLOCK <= BIAS_ROWS and g % gp == 0 and nsp >= nq * (tq // SEL_BLOCK)
    per_bg = lambda n: (lambda bi, gi, i: (bi, gi) + (0,) * n)
    once = pl.Buffered(1)
    return pl.pallas_call(
        functools.partial(_nsa_attn_kernel, tq=tq, n_cmp=n_cmp, gp=gp),
        grid=(b, g // gp, nq),
        in_specs=[
            pl.BlockSpec((1, gp, 1, aug, w), lambda bi, gi, i: (bi, gi, i, 0, 0)),
            pl.BlockSpec((1, gp, ncp, aug), per_bg(2), pipeline_mode=once),
            pl.BlockSpec((1, gp, d, ncp), per_bg(2), pipeline_mode=once),
            pl.BlockSpec((1, gp, nq, tq, aug), per_bg(3), pipeline_mode=once),
            pl.BlockSpec((1, gp, nq, vr, tq), per_bg(3), pipeline_mode=once),
            pl.BlockSpec((1, gp, nq, tq, aug), per_bg(3), pipeline_mode=once),
            pl.BlockSpec((1, gp, nq, vr, tq), per_bg(3), pipeline_mode=once),
            pl.BlockSpec((1, gp, 16, tq), lambda bi, gi, i: (bi, gi, 0, i)),
            pl.BlockSpec((nsp, ncp), lambda bi, gi, i: (0, 0)),
        ],
        out_specs=pl.BlockSpec((1, gp, r, d, tq), lambda bi, gi, i: (bi, gi, 0, 0, i)),
        out_shape=jax.ShapeDtypeStruct((b, g, r, d, s), F32),
        scratch_shapes=[pltpu.VMEM((gp, nq, BIAS_ROWS, w), BF16), pltpu.VMEM((gp, aug, w), BF16),
                        pltpu.VMEM((gp, 1, w), F32), pltpu.VMEM((gp, vr, w), F32),
                        pltpu.VMEM((gp, tq, w), F32), pltpu.VMEM((gp, tq, w), F32),
                        pltpu.VMEM((gp, tq, w), BF16), pltpu.VMEM((gp, tq, w), BF16),
                        pltpu.VMEM((gp, 1, w), F32), pltpu.VMEM((gp, 1, w), F32)],
        compiler_params=_cparams(("parallel", "parallel", "arbitrary")),
        name="nsa_prompt_attn",
    )(qt, kc, vct, ks, vst, kw, vwt, gt, ovt)


def _nsa_sample_kernel(pt_ref, *refs, n_new, p_len, n_cmp, pps):
    del pt_ref
    pg_refs = refs[:pps]
    (qbd_ref, kc_ref, vc_ref, win_ref, ksn_ref, vsn_ref, kwn_ref, vwn_ref, gt_ref, ov_ref, ex_ref, o_ref,
     sel_ref, oc_ref, m_ref, l_ref, acc_ref) = refs[pps:]
    p = pl.program_id(1)
    n_pg = pl.num_programs(1)
    qbd = qbd_ref[0]
    rows = qbd.shape[0]
    tg = rows // NSA_REP
    t_of_row = (lax.broadcasted_iota(jnp.int32, (rows, 1), 0) % tg) // NSA_KV_GROUPS
    pos = p_len + t_of_row

    @pl.when(p == 0)
    def _():
        ncp = kc_ref.shape[1]
        n_id = lax.broadcasted_iota(jnp.int32, (rows, ncp), 1)
        cmask = (n_id * CMP_STRIDE + (CMP_LEN - 1) <= pos) & (n_id < n_cmp)
        pc = _masked_softmax(_dot_nt(qbd, kc_ref[0]), cmask)
        oc_ref[...] = _dot(pc.astype(BF16), vc_ref[0])
        p_sum = pc[0:tg]
        for rr in range(1, NSA_REP):
            p_sum = p_sum + pc[rr * tg:(rr + 1) * tg]
        sel = _nsa_select(p_sum, ov_ref[...], pos[0:tg], ov_ref.shape[-1])
        sel_ref[...] = jnp.concatenate([sel] * NSA_REP, axis=0)
        _reset(m_ref, l_ref, acc_ref)

    kt = jnp.concatenate([r_[0, :KV_WIDTH, :] for r_ in pg_refs], axis=1).astype(BF16)
    vt = jnp.concatenate([r_[0, KV_WIDTH:, :] for r_ in pg_refs], axis=1).astype(BF16)
    chosen = _dot(sel_ref[...].astype(BF16), ex_ref[...]) > 0.5
    _online_update(_dot(qbd, kt), chosen, vt, m_ref, l_ref, acc_ref, v_transposed=True)

    @pl.when(p == n_pg - 1)
    def _():
        t_col = lax.broadcasted_iota(jnp.int32, (rows, NEW_PAD), 1)
        new_ok = (t_col <= t_of_row) & (t_col < n_new)
        _online_update(_dot_nt(qbd, ksn_ref[0]), new_ok, vsn_ref[0], m_ref, l_ref, acc_ref)
        o_s = acc_ref[...] / l_ref[...]
        _reset(m_ref, l_ref, acc_ref)
        win = win_ref[0]
        wb = win.shape[1]
        kw_pos = p_len - wb + lax.broadcasted_iota(jnp.int32, (rows, wb), 1)
        w_ok = (kw_pos <= pos) & (kw_pos > pos - WINDOW) & (kw_pos >= 0)
        _online_update(_dot(qbd, win[:KV_WIDTH].astype(BF16)), w_ok, win[KV_WIDTH:].astype(BF16),
                       m_ref, l_ref, acc_ref, v_transposed=True)
        _online_update(_dot_nt(qbd, kwn_ref[0]), new_ok, vwn_ref[0], m_ref, l_ref, acc_ref)
        o_w = acc_ref[...] / l_ref[...]
        gates = _sigmoid(gt_ref[0])
        o_ref[0] = gates[:, 0:1] * oc_ref[...] + gates[:, 1:2] * o_s + gates[:, 2:3] * o_w


def _nsa_sample(page_table, cache_view, qbd, kc, vc, win, ks_new, vs_new, kw_new, vw_new, gt, ov, n_new, p_len, n_cmp,
                pages_per_step=8):
    db, rows, _ = qbd.shape
    n_pages = page_table.shape[1]
    ncp = kc.shape[1]
    wb = win.shape[2]
    nsp = ov.shape[-1]
    pps = pages_per_step
    assert n_pages % pps == 0
    pt = page_table.reshape(-1)
    per_b = lambda b, p, pt_ref: (b, 0, 0)
    new_spec = pl.BlockSpec((1, NEW_PAD, KV_WIDTH), per_b)
    tile = pps * PAGE_SIZE
    expand = (jnp.arange(nsp)[:, None] == (jnp.arange(n_pages * PAGE_SIZE) // SEL_BLOCK)[None, :]).astype(BF16)

    def page_spec(u):
        return pl.BlockSpec((1, 2 * KV_WIDTH, PAGE_SIZE), lambda b, p, pt_ref: (pt_ref[b * n_pages + p * pps + u], 1, 0))

    grid_spec = pltpu.PrefetchScalarGridSpec(
        num_scalar_prefetch=1,
        grid=(db, n_pages // pps),
        in_specs=[page_spec(u) for u in range(pps)] + [
            pl.BlockSpec((1, rows, KV_WIDTH), per_b),
            pl.BlockSpec((1, ncp, KV_WIDTH), per_b),
            pl.BlockSpec((1, ncp, KV_WIDTH), per_b),
            pl.BlockSpec((1, 2 * KV_WIDTH, wb), per_b),
            new_spec, new_spec, new_spec, new_spec,
            pl.BlockSpec((1, rows, LANES), per_b),
            pl.BlockSpec((ncp, nsp), lambda b, p, pt_ref: (0, 0)),
            pl.BlockSpec((nsp, tile), lambda b, p, pt_ref: (0, p)),
        ],
        out_specs=pl.BlockSpec((1, rows, KV_WIDTH), per_b),
        scratch_shapes=[pltpu.VMEM((rows, nsp), F32), pltpu.VMEM((rows, KV_WIDTH), F32),
                        pltpu.VMEM((rows, 1), F32), pltpu.VMEM((rows, 1), F32), pltpu.VMEM((rows, KV_WIDTH), F32)],
    )
    return pl.pallas_call(
        functools.partial(_nsa_sample_kernel, n_new=n_new, p_len=p_len, n_cmp=n_cmp, pps=pps),
        grid_spec=grid_spec,
        out_shape=jax.ShapeDtypeStruct((db, rows, KV_WIDTH), F32),
        compiler_params=_cparams(("parallel", "arbitrary")),
        name="nsa_sample_attn",
    )(pt, *([cache_view] * pps), qbd, kc, vc, win, ks_new, vs_new, kw_new, vw_new, gt, ov, expand)


def _row_tile(t_rows, tile=512):
    return tile if t_rows % tile == 0 else t_rows


def _pad_rows(a, n):
    return jnp.pad(a, ((0, 0), (0, n - a.shape[1]), (0, 0)))


def _moba_layer(x_p, x_s, cache, page_table, norm_g, w_in, q_g, k_g, w_out):
    b, s, d = x_p.shape
    db, ds, _ = x_s.shape
    h, hd = N_HEADS, HEAD_DIM
    n_pages = page_table.shape[1]
    p_len = n_pages * PAGE_SIZE
    assert s % MOBA_BLOCK == 0 and p_len % MOBA_BLOCK == 0 and ds <= MOBA_BLOCK
    wq, wkv, wz = w_in[:, :MIX_WIDTH], w_in[:, MIX_WIDTH:3 * MIX_WIDTH], w_in[:, 3 * MIX_WIDTH:]
    gq = jnp.tile(q_g, h)
    gkv = jnp.concatenate([jnp.tile(k_g, h), jnp.ones((MIX_WIDTH,), F32)])
    ones = jnp.ones((MIX_WIDTH,), F32)

    def project(x2d, pos):
        tabs = _rope_tables(pos)
        tm = _row_tile(x2d.shape[0], PROJ_ROW_TILE)
        tabs = tabs if pos.shape[0] % tm == 0 else tuple(jnp.tile(t, (tm // pos.shape[0], 1)) for t in tabs)
        q = _proj(x2d, norm_g, wq, [NORM_ROPE] * 4, gq, tabs, tm)
        kv = _proj(x2d, norm_g, wkv, [NORM_ROPE] * 4 + [PLAIN] * 4, gkv, tabs, tm)
        z = _proj(x2d, norm_g, wz, [PLAIN] * 4, ones, tabs, tm)
        return q, kv, z

    xp2 = x_p.reshape(b * s, d)
    q, kv, z = project(xp2, jnp.arange(s, dtype=jnp.int32))
    nb = s // MOBA_BLOCK
    kv3 = kv.reshape(b, s, 2 * MIX_WIDTH)
    km = _block_mean(kv3).reshape(b, nb, h, hd).transpose(0, 2, 1, 3).astype(BF16)
    qt = (q * (SCALE * LOG2E)).astype(BF16).reshape(b, s, h, hd).transpose(0, 2, 3, 1)
    kv5 = kv.astype(BF16).reshape(b, nb, MOBA_BLOCK, 2, h, hd)
    kk = kv5[:, :, :, 0].transpose(0, 3, 1, 2, 4).reshape(b, h, s, hd)
    kk = _aug_keys(kk, jnp.ones((b, h, s, 1), BF16))
    vt = _aug_values_t(kv5[:, :, :, 1].transpose(0, 3, 1, 4, 2))
    o = _moba_attn(_pad_to(qt, 2, AUG), _pad_to(km, 3, AUG), kk, vt).transpose(0, 3, 1, 2).reshape(b * s, MIX_WIDTH)
    y_p = _out_proj(o, z, xp2, w_out, _row_tile(b * s)).reshape(b, s, d)

    xs2 = x_s.reshape(db * ds, d)
    qs, kvs, zs = project(xs2, jnp.tile(p_len + jnp.arange(ds, dtype=jnp.int32), db))
    rows = ds * h
    q_rows = (qs * SCALE).reshape(db, rows, 1, hd)
    head_of_row = jnp.arange(rows) % h
    diag = (head_of_row[:, None] == jnp.arange(h)[None, :]).astype(F32)
    qbd = (q_rows * diag[None, :, :, None]).astype(BF16).reshape(db, rows, MIX_WIDTH)
    kvs3 = kvs.astype(BF16).reshape(db, ds, 2 * MIX_WIDTH)
    k_new = _pad_rows(kvs3[:, :, :MIX_WIDTH], NEW_PAD)
    v_new = _pad_rows(kvs3[:, :, MIX_WIDTH:], NEW_PAD)
    cache_view = jnp.moveaxis(cache, 1, -1).reshape(cache.shape[0], 2 * MIX_WIDTH, PAGE_SIZE)
    o_full = _moba_sample(page_table, cache_view, qbd, k_new, v_new, ds)
    o6 = o_full.reshape(db, ds, h, h, hd)
    o_s = o6[:, :, jnp.arange(h), jnp.arange(h), :].reshape(db * ds, MIX_WIDTH)
    y_s = _out_proj(o_s, zs, xs2, w_out, db * ds).reshape(db, ds, d)
    return (y_p, y_s, kv.reshape(b, s, 2, h, hd), kvs.reshape(db, ds, 2, h, hd))


def _nsa_layer(x_p, x_s, cache, win_past, page_table, norm_g, w_in, q_g, k_g, cmp_pe, cmp_w1, cmp_b1, cmp_w2, w_out):
    b, s, d = x_p.shape
    db, ds, _ = x_s.shape
    h, hd, g, r = N_HEADS, HEAD_DIM, NSA_KV_GROUPS, NSA_REP
    n_pages = page_table.shape[1]
    p_len = n_pages * PAGE_SIZE
    wb = win_past.shape[1]
    tq = WINDOW // 2
    assert s % tq == 0 and s % CMP_STRIDE == 0 and ds <= SEL_BLOCK and p_len % SEL_BLOCK == 0

    o_q, o_rows, o_win = MIX_WIDTH, MIX_WIDTH + 4 * KV_WIDTH, MIX_WIDTH + 6 * KV_WIDTH
    o_z = o_win + 3 * h
    wq, wrows, wwin = w_in[:, :o_q], w_in[:, o_q:o_rows], w_in[:, o_rows:o_win]
    wgt = jnp.pad(w_in[:, o_win:o_z], ((0, 0), (0, LANES - 3 * h)))
    wz = w_in[:, o_z:]
    ones_kv = jnp.ones((KV_WIDTH,), F32)
    gq = jnp.tile(q_g, h)
    grows = jnp.concatenate([ones_kv, ones_kv, jnp.tile(k_g[1], g), ones_kv])
    gwin = jnp.concatenate([jnp.tile(k_g[2], g), ones_kv])

    def project(x2d, pos):
        tabs = _rope_tables(pos)
        tm = _row_tile(x2d.shape[0], PROJ_ROW_TILE)
        tabs = tabs if pos.shape[0] % tm == 0 else tuple(jnp.tile(t, (tm // pos.shape[0], 1)) for t in tabs)
        q = _proj(x2d, norm_g, wq, [NORM_ROPE] * 4, gq, tabs, tm)
        rows_ = _proj(x2d, norm_g, wrows, [ROPE, PLAIN, NORM_ROPE, PLAIN], grows, tabs, tm)
        win_ = _proj(x2d, norm_g, wwin, [NORM_ROPE, PLAIN], gwin, tabs, tm)
        gt_ = _proj(x2d, norm_g, wgt, [PLAIN], jnp.ones((LANES,), F32), tabs, tm, tn=LANES)
        z = _proj(x2d, norm_g, wz, [PLAIN] * 4, jnp.ones((MIX_WIDTH,), F32), tabs, tm)
        return q, rows_, win_, gt_, z

    pe_flat = cmp_pe.reshape(2, 2, CMP_STRIDE * hd)

    xp2 = x_p.reshape(b * s, d)
    q, rows_p, win_p, gt, z = project(xp2, jnp.arange(s, dtype=jnp.int32))
    nr = s // CMP_STRIDE
    n_cmp = (s - CMP_LEN) // CMP_STRIDE + 1
    n_sel = -(-s // SEL_BLOCK)
    nsp = -(-n_sel // LANES) * LANES
    rows6 = rows_p.reshape(b, nr, CMP_STRIDE, 4, g, hd)
    x16 = rows6[:, :, :, :2].transpose(3, 0, 4, 1, 2, 5).reshape(2, b * g, nr, CMP_STRIDE * hd)
    cmp_kv = _compress(x16, pe_flat, cmp_w1, cmp_b1, cmp_w2, k_g[0]).astype(BF16).reshape(2, b, g, nr, hd)
    kc = _pad_to(cmp_kv[0], 3, AUG)
    vct = cmp_kv[1].transpose(0, 1, 3, 2)
    nq = s // tq
    rows_bf = rows_p.astype(BF16).reshape(b, nq, tq, 4, g, hd)
    block_in_tile = jax.nn.one_hot(jnp.arange(tq) // SEL_BLOCK, tq // SEL_BLOCK, dtype=BF16)
    ks = _aug_keys(rows_bf[:, :, :, 2].transpose(0, 3, 1, 2, 4),
                   jnp.broadcast_to(block_in_tile, (b, g, nq, tq, tq // SEL_BLOCK)))
    vst = _aug_values_t(rows_bf[:, :, :, 3].transpose(0, 3, 1, 4, 2))
    win_bf = win_p.astype(BF16).reshape(b, nq, tq, 2, g, hd)
    kw = _pad_to(win_bf[:, :, :, 0].transpose(0, 3, 1, 2, 4), 4, AUG)
    vwt = _aug_values_t(win_bf[:, :, :, 1].transpose(0, 3, 1, 4, 2))
    qt = (q * (SCALE * LOG2E)).astype(BF16).reshape(b, nq, tq, g, r, hd).transpose(0, 3, 1, 5, 4, 2)
    qt = _pad_to(qt.reshape(b, g, nq, hd, r * tq), 3, AUG)
    gt_g = gt[:, :3 * h].reshape(b, s, g, 3 * r).transpose(0, 2, 3, 1)
    gt_g = jnp.pad(gt_g, ((0, 0), (0, 0), (0, 16 - 3 * r), (0, 0)))
    ovt = _overlap(nr, n_cmp, nsp, n_sel).T
    o = _nsa_attn(qt, kc, vct, ks, vst, kw, vwt, gt_g, ovt, n_cmp)
    o = o.transpose(0, 4, 1, 2, 3).reshape(b * s, MIX_WIDTH)
    y_p = _out_proj(o, z, xp2, w_out, _row_tile(b * s)).reshape(b, s, d)
    win_len = min(WINDOW, s)
    win_out_p = win_p.reshape(b, s, 2, g, hd)[:, s - win_len:]

    xs2 = x_s.reshape(db * ds, d)
    qs, rows_s, win_s, gts, zs = project(xs2, jnp.tile(p_len + jnp.arange(ds, dtype=jnp.int32), db))
    t_all = p_len + ds
    nr_s = p_len // CMP_STRIDE
    n_cmp_s = (t_all - CMP_LEN) // CMP_STRIDE + 1
    assert n_cmp_s <= nr_s - 1
    n_sel_s = -(-t_all // SEL_BLOCK)
    nsp_s = -(-n_sel_s // LANES) * LANES
    cache_view = jnp.moveaxis(cache, 1, -1).reshape(cache.shape[0], 4 * KV_WIDTH, PAGE_SIZE)
    x16_s = _page_rows16(page_table, cache_view)
    cmp_s = _compress(x16_s, pe_flat, cmp_w1, cmp_b1, cmp_w2, k_g[0]).astype(BF16).reshape(2, db, g, nr_s, hd)
    cmp_s = cmp_s.transpose(0, 1, 3, 2, 4).reshape(2, db, nr_s, KV_WIDTH)
    n_rows = r * ds * g
    q_rows = (qs * SCALE).reshape(db, ds, g, r, hd).transpose(0, 3, 1, 2, 4).reshape(db, n_rows, 1, hd)
    group_of_row = jnp.arange(n_rows) % g
    diag = (group_of_row[:, None] == jnp.arange(g)[None, :]).astype(F32)
    qbd = (q_rows * diag[None, :, :, None]).astype(BF16).reshape(db, n_rows, KV_WIDTH)
    rows_s3 = rows_s.astype(BF16).reshape(db, ds, 4, KV_WIDTH)
    win_s3 = win_s.astype(BF16).reshape(db, ds, 2, KV_WIDTH)
    new = lambda a: _pad_rows(a, NEW_PAD)
    gts_r = gts[:, :3 * h].reshape(db, ds, g, r, 3).transpose(0, 3, 1, 2, 4).reshape(db, n_rows, 3)
    gts_r = jnp.pad(gts_r, ((0, 0), (0, 0), (0, LANES - 3)))
    ov_s = _overlap(nr_s, n_cmp_s, nsp_s, n_sel_s)
    win_view = jnp.moveaxis(win_past, 1, -1).reshape(db, 2 * KV_WIDTH, wb)
    o_full = _nsa_sample(page_table, cache_view, qbd, cmp_s[0], cmp_s[1], win_view,
                         new(rows_s3[:, :, 2]), new(rows_s3[:, :, 3]), new(win_s3[:, :, 0]), new(win_s3[:, :, 1]),
                         gts_r, ov_s, ds, p_len, n_cmp_s)
    o6 = o_full.reshape(db, r, ds, g, g, hd)
    o_s = o6[:, :, :, jnp.arange(g), jnp.arange(g), :]
    o_s = o_s.transpose(0, 2, 3, 1, 4).reshape(db * ds, MIX_WIDTH)
    y_s = _out_proj(o_s, zs, xs2, w_out, db * ds).reshape(db, ds, d)
    win_new = win_s.reshape(db, ds, 2, g, hd)
    win_out_s = jnp.concatenate([win_past, win_new], axis=1)[:, ds:]
    return (y_p, y_s, rows_p.reshape(b, s, 4, g, hd), rows_s.reshape(db, ds, 4, g, hd), win_out_p, win_out_s)


def kernel(x_prompt, x_sample, cache_moba_kv, cache_nsa_kv, state_nsa_win, page_table, a_norm, a_w_in, a_q_norm,
           a_k_norm, a_w_out, b_norm, b_w_in, b_q_norm, b_k_norm, b_cmp_pe, b_cmp_w1, b_cmp_b1, b_cmp_w2, b_w_out):
    depth = a_norm.shape[0] + b_norm.shape[0]
    n_pool = cache_moba_kv.shape[1]
    moba_pool = cache_moba_kv.reshape((-1,) + cache_moba_kv.shape[2:])
    nsa_pool = cache_nsa_kv.reshape((-1,) + cache_nsa_kv.shape[2:])
    xp, xs = x_prompt, x_sample
    moba_p, moba_s, nsa_p, nsa_s, win_p, win_s = [], [], [], [], [], []
    for layer in range(depth):
        i = layer // 2
        pt = page_table + i * n_pool
        if layer % 2 == 0:
            xp, xs, kvp, kvs = _moba_layer(xp, xs, moba_pool, pt, a_norm[i], a_w_in[i], a_q_norm[i],
                                           a_k_norm[i], a_w_out[i])
            moba_p.append(kvp)
            moba_s.append(kvs)
        else:
            xp, xs, rp, rs, wp, ws = _nsa_layer(xp, xs, nsa_pool, state_nsa_win[i], pt, b_norm[i],
                                                b_w_in[i], b_q_norm[i], b_k_norm[i], b_cmp_pe[i], b_cmp_w1[i],
                                                b_cmp_b1[i], b_cmp_w2[i], b_w_out[i])
            nsa_p.append(rp)
            nsa_s.append(rs)
            win_p.append(wp)
            win_s.append(ws)
    return (xp, xs, jnp.stack(moba_p), jnp.stack(moba_s), jnp.stack(nsa_p), jnp.stack(nsa_s),
            jnp.stack(win_p), jnp.stack(win_s))
```

```python
import functools

import numpy as np
import jax
import jax.numpy as jnp
from jax import lax
from jax.experimental import pallas as pl
from jax.experimental.pallas import tpu as pltpu

F32 = jnp.float32
BF16 = jnp.bfloat16

D_MODEL = 1024
N_HEADS = 16
HEAD_DIM = 64
MIX_WIDTH = N_HEADS * HEAD_DIM
ROT_DIM = HEAD_DIM // 4
ROPE_THETA = 500000.0
NORM_EPS = 1e-6
PAGE_SIZE = 128
MOBA_BLOCK = 256
MOBA_TOPK = 3
NSA_KV_GROUPS = 4
NSA_REP = N_HEADS // NSA_KV_GROUPS
KV_WIDTH = NSA_KV_GROUPS * HEAD_DIM
CMP_LEN = 32
CMP_STRIDE = 16
CMP_HIDDEN = 2 * HEAD_DIM
SEL_BLOCK = 64
SEL_TOPN = 16
WINDOW = 512
SCALE = HEAD_DIM ** -0.5

LANES = 128
NEG = -1e30
LOG2E = 1.4426950408889634
BF16_ROWS = 16
AUG = LANES
BIAS_ROWS = BF16_ROWS
V_ROWS = HEAD_DIM + BF16_ROWS
PROJ_ROW_TILE = 2048
MOBA_HEADS_PER_STEP = 8
M_FLOOR = -1e30
MASKED = -3e38
NEW_PAD = 128
VMEM_LIMIT = 48 * 1024 * 1024

PLAIN, ROPE, NORM_ROPE = 0, 1, 2
NT_DIMS = (((1,), (1,)), ((), ()))


def _cparams(sem):
    return pltpu.CompilerParams(dimension_semantics=sem, vmem_limit_bytes=VMEM_LIMIT)


def _dot(a, b):
    return jnp.dot(a, b, preferred_element_type=F32)


def _dot_nt(a, b):
    return lax.dot_general(a, b, NT_DIMS, preferred_element_type=F32)


def _split_dot(a, b):
    hi = a.astype(BF16)
    lo = (a - hi.astype(F32)).astype(BF16)
    return _dot(hi, b) + _dot(lo, b)


def _proj_kernel(x_ref, g_ref, w_ref, gain_ref, cos_ref, sa_ref, sb_ref, bd_ref, o_ref, xn_ref, *, types, tn):
    j = pl.program_id(1)

    @pl.when(j == 0)
    def _():
        x = x_ref[...]
        ms = jnp.mean(x * x, axis=-1, keepdims=True)
        xn_ref[...] = (x * lax.rsqrt(ms + NORM_EPS) * g_ref[...]).astype(BF16)

    y = _dot(xn_ref[...], w_ref[...])

    def head_norm(v):
        ss = _split_dot(v * v, bd_ref[...])
        return v * lax.rsqrt(ss * (1.0 / HEAD_DIM) + NORM_EPS) * gain_ref[...]

    def rope(v):
        rep = tn // LANES
        c = jnp.concatenate([cos_ref[...]] * rep, axis=1)
        sa = jnp.concatenate([sa_ref[...]] * rep, axis=1)
        sb = jnp.concatenate([sb_ref[...]] * rep, axis=1)
        half = ROT_DIM // 2
        return v * c + pltpu.roll(v, tn - half, 1) * sa + pltpu.roll(v, half, 1) * sb

    def emit(t):
        if t == PLAIN:
            o_ref[...] = y
        elif t == ROPE:
            o_ref[...] = rope(y)
        else:
            o_ref[...] = rope(head_norm(y))

    kinds = sorted(set(types))
    if len(kinds) == 1:
        emit(kinds[0])
    else:
        for t in kinds:
            cond = functools.reduce(jnp.logical_or, [j == jj for jj, tt in enumerate(types) if tt == t])
            pl.when(cond)(functools.partial(emit, t))


def _head_block_diag(tn):
    return jnp.asarray(np.kron(np.eye(tn // HEAD_DIM), np.ones((HEAD_DIM, HEAD_DIM))), BF16)


def _proj(x2d, g, w, types, gain_cols, tabs, tm, tn=256):
    t_rows, d = x2d.shape
    n = w.shape[1]
    assert t_rows % tm == 0 and n % tn == 0 and len(types) == n // tn
    cos, sa, sb = tabs
    assert cos.shape[0] % tm == 0
    ntab = cos.shape[0] // tm
    tab_spec = pl.BlockSpec((tm, LANES), lambda i, j: (i % ntab, 0))
    return pl.pallas_call(
        functools.partial(_proj_kernel, types=tuple(types), tn=tn),
        grid=(t_rows // tm, n // tn),
        in_specs=[
            pl.BlockSpec((tm, d), lambda i, j: (i, 0)),
            pl.BlockSpec((1, d), lambda i, j: (0, 0)),
            pl.BlockSpec((d, tn), lambda i, j: (0, j)),
            pl.BlockSpec((1, tn), lambda i, j: (0, j)),
            tab_spec, tab_spec, tab_spec,
            pl.BlockSpec((tn, tn), lambda i, j: (0, 0)),
        ],
        out_specs=pl.BlockSpec((tm, tn), lambda i, j: (i, j)),
        out_shape=jax.ShapeDtypeStruct((t_rows, n), F32),
        scratch_shapes=[pltpu.VMEM((tm, d), BF16)],
        compiler_params=_cparams(("parallel", "arbitrary")),
        name="rmsnorm_in_proj",
    )(x2d, g.reshape(1, d), w.astype(BF16), gain_cols.reshape(1, n).astype(F32), cos, sa, sb,
      _head_block_diag(tn))


def _rope_tables(pos):
    half = ROT_DIM // 2
    inv_freq = jnp.float32(ROPE_THETA) ** (-jnp.arange(half, dtype=F32) / half)
    ang = pos.astype(F32)[:, None] * inv_freq[None, :]
    cos, sin = jnp.cos(ang), jnp.sin(ang)
    t = pos.shape[0]
    rest = HEAD_DIM - ROT_DIM
    z8, zr, one_r = jnp.zeros((t, half), F32), jnp.zeros((t, rest), F32), jnp.ones((t, rest), F32)
    c = jnp.concatenate([cos, cos, one_r], axis=1)
    sa = jnp.concatenate([-sin, z8, zr], axis=1)
    sb = jnp.concatenate([z8, sin, zr], axis=1)
    rep = LANES // HEAD_DIM
    return tuple(jnp.concatenate([a] * rep, axis=1) for a in (c, sa, sb))


def _out_proj_kernel(o_ref, z_ref, x_ref, w_ref, y_ref):
    z = z_ref[...]
    gated = o_ref[...] * (z * (1.0 / (1.0 + jnp.exp(-z))))
    y_ref[...] = x_ref[...] + _dot(gated.astype(BF16), w_ref[...])


def _out_proj(o2d, z2d, x2d, w_out, tm):
    t_rows, d = x2d.shape
    k = o2d.shape[1]
    row = lambda i: (i, 0)
    return pl.pallas_call(
        _out_proj_kernel,
        grid=(t_rows // tm,),
        in_specs=[pl.BlockSpec((tm, k), row), pl.BlockSpec((tm, k), row), pl.BlockSpec((tm, d), row),
                  pl.BlockSpec((k, d), lambda i: (0, 0))],
        out_specs=pl.BlockSpec((tm, d), row),
        out_shape=jax.ShapeDtypeStruct((t_rows, d), F32),
        compiler_params=_cparams(("parallel",)),
        name="gated_out_proj",
    )(o2d, z2d, x2d, w_out.astype(BF16))


def _top_k_mask(scores, lane, k, width):
    sel = jnp.zeros(scores.shape, F32)
    lane = lane.astype(F32)
    for _ in range(k):
        mx = jnp.max(scores, axis=-1, keepdims=True)
        first = jnp.min(jnp.where(scores == mx, lane, float(width)), axis=-1, keepdims=True)
        hit = lane == first
        sel = jnp.where(hit & (mx > -jnp.inf), 1.0, sel)
        scores = jnp.where(hit, -jnp.inf, scores)
    return sel


def _online_update(s, mask, v, m_ref, l_ref, acc_ref, v_transposed=False):
    s = jnp.where(mask, s, NEG)
    m_old = m_ref[...]
    m_new = jnp.maximum(m_old, jnp.max(s, axis=-1, keepdims=True))
    alpha = jnp.exp(m_old - m_new)
    p = jnp.where(mask, jnp.exp(s - m_new), 0.0)
    l_ref[...] = alpha * l_ref[...] + jnp.sum(p, axis=-1, keepdims=True)
    pv = _dot_nt(p.astype(BF16), v) if v_transposed else _dot(p.astype(BF16), v)
    acc_ref[...] = alpha * acc_ref[...] + pv
    m_ref[...] = m_new


def _reset(m_ref, l_ref, acc_ref):
    m_ref[...] = jnp.full(m_ref.shape, NEG, F32)
    l_ref[...] = jnp.zeros(l_ref.shape, F32)
    acc_ref[...] = jnp.zeros(acc_ref.shape, F32)


def _block_mean_kernel(k_ref, o_ref):
    o_ref[0, 0] = jnp.sum(k_ref[0], axis=0, keepdims=True) * (1.0 / MOBA_BLOCK)


def _block_mean(kv3):
    b, s, _ = kv3.shape
    nb = s // MOBA_BLOCK
    return pl.pallas_call(
        _block_mean_kernel,
        grid=(b, nb),
        in_specs=[pl.BlockSpec((1, MOBA_BLOCK, MIX_WIDTH), lambda i, j: (i, j, 0))],
        out_specs=pl.BlockSpec((1, 1, 1, MIX_WIDTH), lambda i, j: (i, j, 0, 0)),
        out_shape=jax.ShapeDtypeStruct((b, nb, 1, MIX_WIDTH), F32),
        compiler_params=_cparams(("parallel", "parallel")),
        name="moba_block_mean",
    )(kv3)


def _top_k_mask_cols(scores, row, k, height):
    sel = jnp.zeros(scores.shape, F32)
    row = row.astype(F32)
    for _ in range(k):
        mx = jnp.max(scores, axis=0, keepdims=True)
        first = jnp.min(jnp.where(scores == mx, row, float(height)), axis=0, keepdims=True)
        hit = row == first
        sel = jnp.where(hit & (mx > -jnp.inf), 1.0, sel)
        scores = jnp.where(hit, -jnp.inf, scores)
    return sel


def _softmax_stage(s, m_ref, p_ref, a_ref, idx):
    m_old = m_ref[idx]
    m_new = jnp.maximum(m_old, jnp.max(s, axis=0, keepdims=True))
    p_ref[idx] = jnp.exp2(s - m_new).astype(BF16)
    a_ref[idx] = jnp.exp2(m_old - m_new)
    m_ref[idx] = m_new


def _pipelined_flash(n_past, qk, smax, pv):
    qk(0, 0)

    def pair(u, carry):
        t0 = 2 * u
        smax(t0, 0, False)
        qk(t0 + 1, 1)
        pv(jnp.maximum(t0 - 1, 0), 1)
        smax(t0 + 1, 1, False)
        qk(t0 + 2, 0)
        pv(t0, 0)
        return carry

    lax.fori_loop(0, n_past // 2, pair, 0)
    odd = n_past % 2 == 1

    @pl.when(odd)
    def _():
        t0 = n_past - 1
        qk(n_past, 1)
        pv(jnp.maximum(t0 - 1, 0), 1)
        smax(t0, 0, False)
        pv(t0, 0)
        smax(n_past, 1, True)
        pv(n_past, 1)

    @pl.when(jnp.logical_not(odd))
    def _():
        pv(jnp.maximum(n_past - 1, 0), 1)
        smax(n_past, 0, True)
        pv(n_past, 0)


def _bias_rows(keep):
    bias = jnp.where(keep > 0.5, 0.0, MASKED)
    n, width = bias.shape
    if n == 1:
        return jnp.broadcast_to(bias, (BIAS_ROWS, width)).astype(BF16)
    return jnp.concatenate([bias, jnp.zeros((BIAS_ROWS - n, width), F32)], axis=0).astype(BF16)


def _moba_attn_kernel(qt_ref, km_ref, k_ref, vt_ref, o_ref, sel_ref, qa_ref, m_ref, acc_ref,
                      s0_ref, s1_ref, p0_ref, p1_ref, a0_ref, a1_ref, *, hp):
    i = pl.program_id(2)
    blk = MOBA_BLOCK
    nb = km_ref.shape[2]
    s_buf, p_buf, a_buf = (s0_ref, s1_ref), (p0_ref, p1_ref), (a0_ref, a1_ref)
    blk_id = lax.broadcasted_iota(jnp.int32, (nb, blk), 0)
    for hh in range(hp):
        sb = jnp.where(blk_id < i, _dot(km_ref[0, hh], qt_ref[0, hh]), -jnp.inf)
        sel_ref[hh] = _top_k_mask_cols(sb, blk_id, min(MOBA_TOPK, nb), nb)
    qa_ref[...] = qt_ref[0]
    m_ref[...] = jnp.full(m_ref.shape, M_FLOOR, F32)
    acc_ref[...] = jnp.zeros(acc_ref.shape, F32)
    p1_ref[...] = jnp.zeros(p1_ref.shape, BF16)
    a1_ref[...] = jnp.ones(a1_ref.shape, F32)
    causal = lax.broadcasted_iota(jnp.int32, (blk, blk), 0) <= lax.broadcasted_iota(jnp.int32, (blk, blk), 1)

    def qk(t, slot):
        rows = pl.ds(pl.multiple_of(t * blk, blk), blk)
        for hh in range(hp):
            keep = jnp.where(t == i, 1.0, sel_ref[hh, pl.ds(t, 1), :])
            qa_ref[hh, HEAD_DIM:HEAD_DIM + BIAS_ROWS, :] = _bias_rows(keep)
            s_buf[slot][hh] = _dot(k_ref[0, hh, rows, :], qa_ref[hh])

    def smax(t, slot, diagonal):
        for hh in range(hp):
            s = s_buf[slot][hh]
            if diagonal:
                s = jnp.where(causal, s, MASKED)
            _softmax_stage(s, m_ref, p_buf[slot], a_buf[slot], hh)

    def pv(t, slot):
        for hh in range(hp):
            acc_ref[hh] = a_buf[slot][hh] * acc_ref[hh] + _dot(vt_ref[0, hh, t], p_buf[slot][hh])

    _pipelined_flash(i, qk, smax, pv)
    for hh in range(hp):
        o_ref[0, hh] = acc_ref[hh, :HEAD_DIM, :] / acc_ref[hh, HEAD_DIM:HEAD_DIM + 1, :]


def _moba_attn(qt, km, k, vt, hp=MOBA_HEADS_PER_STEP):
    b, h, aug, s = qt.shape
    d = HEAD_DIM
    nb = s // MOBA_BLOCK
    blk = MOBA_BLOCK
    vr = vt.shape[3]
    assert h % hp == 0
    per_bh = lambda n: (lambda bi, hi, i: (bi, hi) + (0,) * n)
    return pl.pallas_call(
        functools.partial(_moba_attn_kernel, hp=hp),
        grid=(b, h // hp, nb),
        in_specs=[
            pl.BlockSpec((1, hp, aug, blk), lambda bi, hi, i: (bi, hi, 0, i)),
            pl.BlockSpec((1, hp, nb, aug), per_bh(2)),
            pl.BlockSpec((1, hp, s, aug), per_bh(2), pipeline_mode=pl.Buffered(1)),
            pl.BlockSpec((1, hp, nb, vr, blk), per_bh(3), pipeline_mode=pl.Buffered(1)),
        ],
        out_specs=pl.BlockSpec((1, hp, d, blk), lambda bi, hi, i: (bi, hi, 0, i)),
        out_shape=jax.ShapeDtypeStruct((b, h, d, s), F32),
        scratch_shapes=[pltpu.VMEM((hp, nb, blk), F32), pltpu.VMEM((hp, aug, blk), BF16), pltpu.VMEM((hp, 1, blk), F32),
                        pltpu.VMEM((hp, vr, blk), F32),
                        pltpu.VMEM((hp, blk, blk), F32), pltpu.VMEM((hp, blk, blk), F32),
                        pltpu.VMEM((hp, blk, blk), BF16), pltpu.VMEM((hp, blk, blk), BF16),
                        pltpu.VMEM((hp, 1, blk), F32), pltpu.VMEM((hp, 1, blk), F32)],
        compiler_params=_cparams(("parallel", "parallel", "arbitrary")),
        name="moba_prompt_attn",
    )(qt, km, k, vt)


def _aug_keys(k, extra):
    pad = AUG - k.shape[-1] - extra.shape[-1]
    return jnp.concatenate([k, extra.astype(k.dtype), jnp.zeros(k.shape[:-1] + (pad,), k.dtype)], axis=-1)


def _aug_values_t(vt):
    shape = vt.shape[:-2]
    keys = vt.shape[-1]
    return jnp.concatenate([vt, jnp.ones(shape + (1, keys), vt.dtype),
                            jnp.zeros(shape + (V_ROWS - vt.shape[-2] - 1, keys), vt.dtype)], axis=-2)


def _pad_to(a, axis, size):
    pad = [(0, 0)] * a.ndim
    pad[axis] = (0, size - a.shape[axis])
    return jnp.pad(a, pad)


def _moba_sample_kernel(pt_ref, *refs, n_new, bps):
    del pt_ref
    ppb = MOBA_BLOCK // PAGE_SIZE
    k_refs, v_refs = refs[:bps * ppb], refs[bps * ppb:2 * bps * ppb]
    qbd_ref, kn_ref, vn_ref, o_ref, ob_ref, mb_ref, lb_ref, sc_ref, w_ref = refs[2 * bps * ppb:]
    j = pl.program_id(1)
    nbp = pl.num_programs(1)
    qbd = qbd_ref[0]
    rows = qbd.shape[0]
    for bb in range(bps):
        blk = j * bps + bb
        s = jnp.concatenate([_dot(qbd, k_refs[bb * ppb + u][0].astype(BF16)) for u in range(ppb)], axis=1)
        sc_ref[blk] = jnp.sum(s, axis=-1, keepdims=True) * (1.0 / MOBA_BLOCK)
        m = jnp.max(s, axis=-1, keepdims=True)
        p = jnp.exp(s - m)
        mb_ref[blk] = m
        lb_ref[blk] = jnp.sum(p, axis=-1, keepdims=True)
        pb = p.astype(BF16)
        ob_ref[blk] = sum(_dot_nt(pb[:, u * PAGE_SIZE:(u + 1) * PAGE_SIZE], v_refs[bb * ppb + u][0].astype(BF16))
                          for u in range(ppb))

    @pl.when(j == nbp - 1)
    def _():
        nb = sc_ref.shape[0]
        blk_id = lax.broadcasted_iota(jnp.int32, sc_ref.shape, 0).astype(F32)
        sc = sc_ref[...]
        sel = jnp.zeros(sc.shape, F32)
        for _ in range(MOBA_TOPK):
            mx = jnp.max(sc, axis=0, keepdims=True)
            first = jnp.min(jnp.where(sc == mx, blk_id, float(nb)), axis=0, keepdims=True)
            hit = blk_id == first
            sel = jnp.where(hit, 1.0, sel)
            sc = jnp.where(hit, -jnp.inf, sc)
        s_new = _dot_nt(qbd, kn_ref[0])
        t_row = lax.broadcasted_iota(jnp.int32, s_new.shape, 0) // N_HEADS
        t_col = lax.broadcasted_iota(jnp.int32, s_new.shape, 1)
        ok = (t_col <= t_row) & (t_col < n_new)
        m_new = jnp.max(jnp.where(ok, s_new, NEG), axis=-1, keepdims=True)
        p_new = jnp.where(ok, jnp.exp(s_new - m_new), 0.0)
        l_new = jnp.sum(p_new, axis=-1, keepdims=True)
        o_new = _dot(p_new.astype(BF16), vn_ref[0])
        picked = sel > 0.5
        m_all = jnp.maximum(jnp.max(jnp.where(picked, mb_ref[...], NEG), axis=0), m_new)
        w = jnp.where(picked, jnp.exp(mb_ref[...] - m_all[None]), 0.0)
        w_ref[...] = w
        w_new = jnp.exp(m_new - m_all)
        den = jnp.sum(w * lb_ref[...], axis=0) + w_new * l_new

        def add_block(n, acc):
            return acc + w_ref[n] * ob_ref[n]

        num = lax.fori_loop(0, nb, add_block, w_new * o_new)
        o_ref[0] = num / den


def _moba_sample(page_table, cache_view, qbd, k_new, v_new, n_new, blocks_per_step=4):
    db, rows, _ = qbd.shape
    n_pages = page_table.shape[1]
    ppb = MOBA_BLOCK // PAGE_SIZE
    bps = blocks_per_step
    assert n_pages % (ppb * bps) == 0
    nbp = n_pages // ppb
    assert nbp >= MOBA_TOPK
    pt = page_table.reshape(-1)
    pages_per_step = ppb * bps

    def page_spec(u, half):
        return pl.BlockSpec((1, MIX_WIDTH, PAGE_SIZE),
                            lambda b, j, pt_ref: (pt_ref[b * n_pages + pages_per_step * j + u], half, 0))

    per_b = lambda b, j, pt_ref: (b, 0, 0)
    grid_spec = pltpu.PrefetchScalarGridSpec(
        num_scalar_prefetch=1,
        grid=(db, nbp // bps),
        in_specs=[page_spec(u, 0) for u in range(pages_per_step)] + [page_spec(u, 1) for u in range(pages_per_step)] + [
                  pl.BlockSpec((1, rows, MIX_WIDTH), per_b),
                  pl.BlockSpec((1, NEW_PAD, MIX_WIDTH), per_b),
                  pl.BlockSpec((1, NEW_PAD, MIX_WIDTH), per_b)],
        out_specs=pl.BlockSpec((1, rows, MIX_WIDTH), per_b),
        scratch_shapes=[pltpu.VMEM((nbp, rows, MIX_WIDTH), F32), pltpu.VMEM((nbp, rows, 1), F32),
                        pltpu.VMEM((nbp, rows, 1), F32), pltpu.VMEM((nbp, rows, 1), F32),
                        pltpu.VMEM((nbp, rows, 1), F32)],
    )
    return pl.pallas_call(
        functools.partial(_moba_sample_kernel, n_new=n_new, bps=bps),
        grid_spec=grid_spec,
        out_shape=jax.ShapeDtypeStruct((db, rows, MIX_WIDTH), F32),
        compiler_params=_cparams(("parallel", "arbitrary")),
        name="moba_sample_attn",
    )(pt, *([cache_view] * (2 * pages_per_step)), qbd, k_new, v_new)


def _compress_kernel(x_ref, pe_ref, w1_ref, b1_ref, w2_ref, kg_ref, o_ref):
    c = pl.program_id(0)
    x = x_ref[0, 0]
    nr = x.shape[0]
    half = x.shape[1]
    pe = pe_ref[0]
    p0 = _dot((x + pe[0:1]).astype(BF16), w1_ref[0, :half, :])
    p1 = _dot((x + pe[1:2]).astype(BF16), w1_ref[0, half:, :])
    h = p0 + pltpu.roll(p1, nr - 1, 0) + b1_ref[0]
    g = 0.5 * h * (1.0 + jnp.tanh(0.7978845608028654 * (h + 0.044715 * (h * h * h))))
    y = _dot(g.astype(BF16), w2_ref[0])
    yn = y * lax.rsqrt(jnp.mean(y * y, axis=-1, keepdims=True) + NORM_EPS) * kg_ref[...]
    y = jnp.where(c == 0, yn, y)
    row = lax.broadcasted_iota(jnp.int32, y.shape, 0)
    o_ref[0, 0] = jnp.where(row < nr - 1, y, 0.0)


def _compress(x16, pe, w1, b1, w2, kg0):
    _, bb, nr, width = x16.shape
    per_c = lambda c, i: (c, 0, 0)
    return pl.pallas_call(
        _compress_kernel,
        grid=(2, bb),
        in_specs=[
            pl.BlockSpec((1, 1, nr, width), lambda c, i: (c, i, 0, 0)),
            pl.BlockSpec((1, 2, width), per_c),
            pl.BlockSpec((1, CMP_LEN * HEAD_DIM, CMP_HIDDEN), per_c),
            pl.BlockSpec((1, 1, CMP_HIDDEN), per_c),
            pl.BlockSpec((1, CMP_HIDDEN, HEAD_DIM), per_c),
            pl.BlockSpec((1, HEAD_DIM), lambda c, i: (0, 0)),
        ],
        out_specs=pl.BlockSpec((1, 1, nr, HEAD_DIM), lambda c, i: (c, i, 0, 0)),
        out_shape=jax.ShapeDtypeStruct((2, bb, nr, HEAD_DIM), F32),
        compiler_params=_cparams(("parallel", "parallel")),
        name="nsa_compress",
    )(x16, pe.reshape(2, 2, width), w1.astype(BF16), b1.reshape(2, 1, CMP_HIDDEN), w2.astype(BF16),
      kg0.reshape(1, HEAD_DIM))


def _page_rows16_kernel(pt_ref, *refs, n_in):
    del pt_ref
    o_ref, t_ref = refs[n_in], refs[n_in + 1]
    rows_per_page = PAGE_SIZE // CMP_STRIDE
    pair = LANES // HEAD_DIM
    for u in range(n_in):
        for comp in range(2):
            for gp in range(NSA_KV_GROUPS // pair):
                slot = (u * 2 + comp) * (NSA_KV_GROUPS // pair) + gp
                r0 = comp * KV_WIDTH + gp * LANES
                t_ref[slot] = refs[u][0, r0:r0 + LANES, :].T
                for c in range(CMP_STRIDE):
                    v = t_ref[slot, pl.ds(c, rows_per_page, stride=CMP_STRIDE), :]
                    for k in range(pair):
                        o_ref[comp, gp * pair + k, u * rows_per_page:(u + 1) * rows_per_page,
                              c * HEAD_DIM:(c + 1) * HEAD_DIM] = v[:, k * HEAD_DIM:(k + 1) * HEAD_DIM]


def _page_rows16(page_table, cache_view, pages_per_step=16):
    db, n_pages = page_table.shape
    assert n_pages % pages_per_step == 0 and PAGE_SIZE == LANES
    rows_per_page = PAGE_SIZE // CMP_STRIDE
    pt = page_table.reshape(-1)

    def page_spec(u):
        return pl.BlockSpec((1, 2 * KV_WIDTH, PAGE_SIZE),
                            lambda b, p, pt_ref: (pt_ref[b * n_pages + p * pages_per_step + u], 0, 0))

    n_slots = pages_per_step * 2 * (NSA_KV_GROUPS * HEAD_DIM // LANES)
    grid_spec = pltpu.PrefetchScalarGridSpec(
        num_scalar_prefetch=1,
        grid=(db, n_pages // pages_per_step),
        in_specs=[page_spec(u) for u in range(pages_per_step)],
        out_specs=pl.BlockSpec((2, NSA_KV_GROUPS, pages_per_step * rows_per_page, CMP_STRIDE * HEAD_DIM),
                               lambda b, p, pt_ref: (0, b, p, 0)),
        scratch_shapes=[pltpu.VMEM((n_slots, PAGE_SIZE, LANES), F32)],
    )
    return pl.pallas_call(
        functools.partial(_page_rows16_kernel, n_in=pages_per_step),
        grid_spec=grid_spec,
        out_shape=jax.ShapeDtypeStruct((2, db * NSA_KV_GROUPS, n_pages * rows_per_page, CMP_STRIDE * HEAD_DIM), F32),
        compiler_params=_cparams(("parallel", "parallel")),
        name="nsa_page_rows16",
    )(pt, *([cache_view] * pages_per_step))


def _overlap(n_rows, n_cmp, n_cols, n_sel):
    i = np.arange(n_rows)[:, None]
    j = np.arange(n_cols)[None, :]
    st = i * CMP_STRIDE
    j0 = j * SEL_BLOCK
    ov = (st < j0 + SEL_BLOCK) & (st + CMP_LEN > j0) & (i < n_cmp) & (j < n_sel)
    return jnp.asarray(ov, BF16)


def _sigmoid(x):
    return 1.0 / (1.0 + jnp.exp(-x))


def _nsa_select(p_sum, ov, pos, n_cols):
    imp = _split_dot(p_sum, ov)
    j = lax.broadcasted_iota(jnp.int32, imp.shape, 1)
    cur = pos // SEL_BLOCK
    forced = (j == 0) | (j == cur) | (j == cur - 1)
    imp = jnp.where(forced, jnp.inf, jnp.where(j <= cur, imp, -jnp.inf))
    return _top_k_mask(imp, j, min(SEL_TOPN, n_cols), n_cols)


def _masked_softmax(s, mask):
    s = jnp.where(mask, s, NEG)
    m = jnp.max(s, axis=-1, keepdims=True)
    p = jnp.where(mask, jnp.exp(s - m), 0.0)
    return p / jnp.maximum(jnp.sum(p, axis=-1, keepdims=True), 1e-30)


def _nsa_attn_kernel(qt_ref, kc_ref, vct_ref, ks_ref, vst_ref, kw_ref, vwt_ref, gt_ref, ovt_ref, o_ref,
                     bias_ref, qa_ref, m_ref, acc_ref, s0_ref, s1_ref, p0_ref, p1_ref, a0_ref, a1_ref,
                     *, tq, n_cmp, gp):
    i = pl.program_id(2)
    r = NSA_REP
    nq = ks_ref.shape[2]
    pos = i * tq + lax.broadcasted_iota(jnp.int32, (1, tq), 1)
    key_id = lax.broadcasted_iota(jnp.int32, (tq, tq), 0)
    q_id = lax.broadcasted_iota(jnp.int32, (tq, tq), 1)
    causal = key_id <= q_id
    cols = [slice(rr * tq, (rr + 1) * tq) for rr in range(r)]
    blocks_per_tile = tq // SEL_BLOCK
    ncp = kc_ref.shape[2]
    nsp = ovt_ref.shape[0]
    n_id = lax.broadcasted_iota(jnp.int32, (ncp, tq), 0)
    cmask = (n_id * CMP_STRIDE + (CMP_LEN - 1) <= pos) & (n_id < n_cmp)
    j = lax.broadcasted_iota(jnp.int32, (nsp, tq), 0)
    cur = pos // SEL_BLOCK
    forced = (j == 0) | (j == cur) | (j == cur - 1)
    for gi in range(gp):
        qt = qt_ref[0, gi, 0]
        sc = _dot(kc_ref[0, gi], qt)
        p_sum = jnp.zeros((ncp, tq), F32)
        pcs = []
        for rr in range(r):
            s = jnp.where(cmask, sc[:, cols[rr]], MASKED)
            m = jnp.maximum(jnp.max(s, axis=0, keepdims=True), M_FLOOR)
            p = jnp.exp2(s - m)
            pc = p * (1.0 / jnp.maximum(jnp.sum(p, axis=0, keepdims=True), 1e-30))
            p_sum = p_sum + pc
            pcs.append(pc.astype(BF16))
        o_c = _dot(vct_ref[0, gi], jnp.concatenate(pcs, axis=1))

        hi = p_sum.astype(BF16)
        lo = (p_sum - hi.astype(F32)).astype(BF16)
        imp = _dot(ovt_ref[...], hi) + _dot(ovt_ref[...], lo)
        imp = jnp.where(forced, jnp.inf, jnp.where(j <= cur, imp, -jnp.inf))
        sel = _top_k_mask_cols(imp, j, min(SEL_TOPN, nsp), nsp)
        for t in range(nq):
            rows = _bias_rows(sel[t * blocks_per_tile:(t + 1) * blocks_per_tile])
            bias_ref[gi, t] = jnp.concatenate([rows] * r, axis=1)

        w_tiles = []
        for back, wmask in ((2, key_id > q_id), (1, None), (0, causal)):
            live = i >= back
            wt = jnp.maximum(i - back, 0)
            mask = live if wmask is None else wmask & live
            w_tiles.append((wt, mask, _dot(kw_ref[0, gi, wt], qt)))
        gates = _sigmoid(gt_ref[0, gi])
        for rr in range(r):
            ss = [jnp.where(mask, s[:, cols[rr]], MASKED) for _, mask, s in w_tiles]
            m = ss[0].max(axis=0, keepdims=True)
            for s in ss[1:]:
                m = jnp.maximum(m, s.max(axis=0, keepdims=True))
            num = sum(_dot(vwt_ref[0, gi, wt], jnp.exp2(s - m).astype(BF16)) for (wt, _, _), s in zip(w_tiles, ss))
            o_w = num[:HEAD_DIM] / num[HEAD_DIM:HEAD_DIM + 1]
            o_ref[0, gi, rr] = gates[3 * rr:3 * rr + 1] * o_c[:, cols[rr]] + gates[3 * rr + 2:3 * rr + 3] * o_w
        qa_ref[gi] = qt

    s_buf, p_buf, a_buf = (s0_ref, s1_ref), (p0_ref, p1_ref), (a0_ref, a1_ref)
    m_ref[...] = jnp.full(m_ref.shape, M_FLOOR, F32)
    acc_ref[...] = jnp.zeros(acc_ref.shape, F32)
    p1_ref[...] = jnp.zeros(p1_ref.shape, BF16)
    a1_ref[...] = jnp.ones(a1_ref.shape, F32)

    def qk(t, slot):
        for gi in range(gp):
            qa_ref[gi, HEAD_DIM:HEAD_DIM + BIAS_ROWS, :] = bias_ref[gi, t]
            s_buf[slot][gi] = _dot(ks_ref[0, gi, t], qa_ref[gi])

    def smax(t, slot, diagonal):
        for gi in range(gp):
            for rr in range(r):
                idx = (gi, slice(None), cols[rr])
                s = s_buf[slot][idx]
                if diagonal:
                    s = jnp.where(causal, s, MASKED)
                _softmax_stage(s, m_ref, p_buf[slot], a_buf[slot], idx)

    def pv(t, slot):
        for gi in range(gp):
            acc_ref[gi] = a_buf[slot][gi] * acc_ref[gi] + _dot(vst_ref[0, gi, t], p_buf[slot][gi])

    _pipelined_flash(i, qk, smax, pv)

    for gi in range(gp):
        o_s = acc_ref[gi, :HEAD_DIM, :] / acc_ref[gi, HEAD_DIM:HEAD_DIM + 1, :]
        gates = _sigmoid(gt_ref[0, gi])
        for rr in range(r):
            o_ref[0, gi, rr] += gates[3 * rr + 1:3 * rr + 2] * o_s[:, cols[rr]]


def _nsa_attn(qt, kc, vct, ks, vst, kw, vwt, gt, ovt, n_cmp, gp=2):
    b, g, nq, aug, w = qt.shape
    r = NSA_REP
    d = HEAD_DIM
    tq = w // r
    s = nq * tq
    ncp = kc.shape[2]
    nsp = ovt.shape[0]
    vr = vst.shape[3]
    assert ks.shape[2:] == (nq, tq, aug) and kw.shape[2:] == (nq, tq, aug) and WINDOW == 2 * tq
    assert tq % SEL_BLOCK == 0 and tq // SEL_BLOCK <= BIAS_ROWS and g % gp == 0 and nsp >= nq * (tq // SEL_BLOCK)
    per_bg = lambda n: (lambda bi, gi, i: (bi, gi) + (0,) * n)
    once = pl.Buffered(1)
    return pl.pallas_call(
        functools.partial(_nsa_attn_kernel, tq=tq, n_cmp=n_cmp, gp=gp),
        grid=(b, g // gp, nq),
        in_specs=[
            pl.BlockSpec((1, gp, 1, aug, w), lambda bi, gi, i: (bi, gi, i, 0, 0)),
            pl.BlockSpec((1, gp, ncp, aug), per_bg(2), pipeline_mode=once),
            pl.BlockSpec((1, gp, d, ncp), per_bg(2), pipeline_mode=once),
            pl.BlockSpec((1, gp, nq, tq, aug), per_bg(3), pipeline_mode=once),
            pl.BlockSpec((1, gp, nq, vr, tq), per_bg(3), pipeline_mode=once),
            pl.BlockSpec((1, gp, nq, tq, aug), per_bg(3), pipeline_mode=once),
            pl.BlockSpec((1, gp, nq, vr, tq), per_bg(3), pipeline_mode=once),
            pl.BlockSpec((1, gp, 16, tq), lambda bi, gi, i: (bi, gi, 0, i)),
            pl.BlockSpec((nsp, ncp), lambda bi, gi, i: (0, 0)),
        ],
        out_specs=pl.BlockSpec((1, gp, r, d, tq), lambda bi, gi, i: (bi, gi, 0, 0, i)),
        out_shape=jax.ShapeDtypeStruct((b, g, r, d, s), F32),
        scratch_shapes=[pltpu.VMEM((gp, nq, BIAS_ROWS, w), BF16), pltpu.VMEM((gp, aug, w), BF16),
                        pltpu.VMEM((gp, 1, w), F32), pltpu.VMEM((gp, vr, w), F32),
                        pltpu.VMEM((gp, tq, w), F32), pltpu.VMEM((gp, tq, w), F32),
                        pltpu.VMEM((gp, tq, w), BF16), pltpu.VMEM((gp, tq, w), BF16),
                        pltpu.VMEM((gp, 1, w), F32), pltpu.VMEM((gp, 1, w), F32)],
        compiler_params=_cparams(("parallel", "parallel", "arbitrary")),
        name="nsa_prompt_attn",
    )(qt, kc, vct, ks, vst, kw, vwt, gt, ovt)


def _nsa_sample_kernel(pt_ref, *refs, n_new, p_len, n_cmp, pps):
    del pt_ref
    pg_refs = refs[:pps]
    (qbd_ref, kc_ref, vc_ref, win_ref, ksn_ref, vsn_ref, kwn_ref, vwn_ref, gt_ref, ov_ref, ex_ref, o_ref,
     sel_ref, oc_ref, m_ref, l_ref, acc_ref) = refs[pps:]
    p = pl.program_id(1)
    n_pg = pl.num_programs(1)
    qbd = qbd_ref[0]
    rows = qbd.shape[0]
    tg = rows // NSA_REP
    t_of_row = (lax.broadcasted_iota(jnp.int32, (rows, 1), 0) % tg) // NSA_KV_GROUPS
    pos = p_len + t_of_row

    @pl.when(p == 0)
    def _():
        ncp = kc_ref.shape[1]
        n_id = lax.broadcasted_iota(jnp.int32, (rows, ncp), 1)
        cmask = (n_id * CMP_STRIDE + (CMP_LEN - 1) <= pos) & (n_id < n_cmp)
        pc = _masked_softmax(_dot_nt(qbd, kc_ref[0]), cmask)
        oc_ref[...] = _dot(pc.astype(BF16), vc_ref[0])
        p_sum = pc[0:tg]
        for rr in range(1, NSA_REP):
            p_sum = p_sum + pc[rr * tg:(rr + 1) * tg]
        sel = _nsa_select(p_sum, ov_ref[...], pos[0:tg], ov_ref.shape[-1])
        sel_ref[...] = jnp.concatenate([sel] * NSA_REP, axis=0)
        _reset(m_ref, l_ref, acc_ref)

    kt = jnp.concatenate([r_[0, :KV_WIDTH, :] for r_ in pg_refs], axis=1).astype(BF16)
    vt = jnp.concatenate([r_[0, KV_WIDTH:, :] for r_ in pg_refs], axis=1).astype(BF16)
    chosen = _dot(sel_ref[...].astype(BF16), ex_ref[...]) > 0.5
    _online_update(_dot(qbd, kt), chosen, vt, m_ref, l_ref, acc_ref, v_transposed=True)

    @pl.when(p == n_pg - 1)
    def _():
        t_col = lax.broadcasted_iota(jnp.int32, (rows, NEW_PAD), 1)
        new_ok = (t_col <= t_of_row) & (t_col < n_new)
        _online_update(_dot_nt(qbd, ksn_ref[0]), new_ok, vsn_ref[0], m_ref, l_ref, acc_ref)
        o_s = acc_ref[...] / l_ref[...]
        _reset(m_ref, l_ref, acc_ref)
        win = win_ref[0]
        wb = win.shape[1]
        kw_pos = p_len - wb + lax.broadcasted_iota(jnp.int32, (rows, wb), 1)
        w_ok = (kw_pos <= pos) & (kw_pos > pos - WINDOW) & (kw_pos >= 0)
        _online_update(_dot(qbd, win[:KV_WIDTH].astype(BF16)), w_ok, win[KV_WIDTH:].astype(BF16),
                       m_ref, l_ref, acc_ref, v_transposed=True)
        _online_update(_dot_nt(qbd, kwn_ref[0]), new_ok, vwn_ref[0], m_ref, l_ref, acc_ref)
        o_w = acc_ref[...] / l_ref[...]
        gates = _sigmoid(gt_ref[0])
        o_ref[0] = gates[:, 0:1] * oc_ref[...] + gates[:, 1:2] * o_s + gates[:, 2:3] * o_w


def _nsa_sample(page_table, cache_view, qbd, kc, vc, win, ks_new, vs_new, kw_new, vw_new, gt, ov, n_new, p_len, n_cmp,
                pages_per_step=16):
    db, rows, _ = qbd.shape
    n_pages = page_table.shape[1]
    ncp = kc.shape[1]
    wb = win.shape[2]
    nsp = ov.shape[-1]
    pps = pages_per_step
    assert n_pages % pps == 0
    pt = page_table.reshape(-1)
    per_b = lambda b, p, pt_ref: (b, 0, 0)
    new_spec = pl.BlockSpec((1, NEW_PAD, KV_WIDTH), per_b)
    tile = pps * PAGE_SIZE
    expand = (jnp.arange(nsp)[:, None] == (jnp.arange(n_pages * PAGE_SIZE) // SEL_BLOCK)[None, :]).astype(BF16)

    def page_spec(u):
        return pl.BlockSpec((1, 2 * KV_WIDTH, PAGE_SIZE), lambda b, p, pt_ref: (pt_ref[b * n_pages + p * pps + u], 1, 0))

    grid_spec = pltpu.PrefetchScalarGridSpec(
        num_scalar_prefetch=1,
        grid=(db, n_pages // pps),
        in_specs=[page_spec(u) for u in range(pps)] + [
            pl.BlockSpec((1, rows, KV_WIDTH), per_b),
            pl.BlockSpec((1, ncp, KV_WIDTH), per_b),
            pl.BlockSpec((1, ncp, KV_WIDTH), per_b),
            pl.BlockSpec((1, 2 * KV_WIDTH, wb), per_b),
            new_spec, new_spec, new_spec, new_spec,
            pl.BlockSpec((1, rows, LANES), per_b),
            pl.BlockSpec((ncp, nsp), lambda b, p, pt_ref: (0, 0)),
            pl.BlockSpec((nsp, tile), lambda b, p, pt_ref: (0, p)),
        ],
        out_specs=pl.BlockSpec((1, rows, KV_WIDTH), per_b),
        scratch_shapes=[pltpu.VMEM((rows, nsp), F32), pltpu.VMEM((rows, KV_WIDTH), F32),
                        pltpu.VMEM((rows, 1), F32), pltpu.VMEM((rows, 1), F32), pltpu.VMEM((rows, KV_WIDTH), F32)],
    )
    return pl.pallas_call(
        functools.partial(_nsa_sample_kernel, n_new=n_new, p_len=p_len, n_cmp=n_cmp, pps=pps),
        grid_spec=grid_spec,
        out_shape=jax.ShapeDtypeStruct((db, rows, KV_WIDTH), F32),
        compiler_params=_cparams(("parallel", "arbitrary")),
        name="nsa_sample_attn",
    )(pt, *([cache_view] * pps), qbd, kc, vc, win, ks_new, vs_new, kw_new, vw_new, gt, ov, expand)


def _row_tile(t_rows, tile=512):
    return tile if t_rows % tile == 0 else t_rows


def _pad_rows(a, n):
    return jnp.pad(a, ((0, 0), (0, n - a.shape[1]), (0, 0)))


def _moba_layer(x_p, x_s, cache, page_table, norm_g, w_in, q_g, k_g, w_out):
    b, s, d = x_p.shape
    db, ds, _ = x_s.shape
    h, hd = N_HEADS, HEAD_DIM
    n_pages = page_table.shape[1]
    p_len = n_pages * PAGE_SIZE
    assert s % MOBA_BLOCK == 0 and p_len % MOBA_BLOCK == 0 and ds <= MOBA_BLOCK
    wq, wkv, wz = w_in[:, :MIX_WIDTH], w_in[:, MIX_WIDTH:3 * MIX_WIDTH], w_in[:, 3 * MIX_WIDTH:]
    gq = jnp.tile(q_g, h)
    gkv = jnp.concatenate([jnp.tile(k_g, h), jnp.ones((MIX_WIDTH,), F32)])
    ones = jnp.ones((MIX_WIDTH,), F32)

    def project(x2d, pos):
        tabs = _rope_tables(pos)
        tm = _row_tile(x2d.shape[0], PROJ_ROW_TILE)
        tabs = tabs if pos.shape[0] % tm == 0 else tuple(jnp.tile(t, (tm // pos.shape[0], 1)) for t in tabs)
        q = _proj(x2d, norm_g, wq, [NORM_ROPE] * 4, gq, tabs, tm)
        kv = _proj(x2d, norm_g, wkv, [NORM_ROPE] * 4 + [PLAIN] * 4, gkv, tabs, tm)
        z = _proj(x2d, norm_g, wz, [PLAIN] * 4, ones, tabs, tm)
        return q, kv, z

    xp2 = x_p.reshape(b * s, d)
    q, kv, z = project(xp2, jnp.arange(s, dtype=jnp.int32))
    nb = s // MOBA_BLOCK
    kv3 = kv.reshape(b, s, 2 * MIX_WIDTH)
    km = _block_mean(kv3).reshape(b, nb, h, hd).transpose(0, 2, 1, 3).astype(BF16)
    qt = (q * (SCALE * LOG2E)).astype(BF16).reshape(b, s, h, hd).transpose(0, 2, 3, 1)
    kv5 = kv.astype(BF16).reshape(b, nb, MOBA_BLOCK, 2, h, hd)
    kk = kv5[:, :, :, 0].transpose(0, 3, 1, 2, 4).reshape(b, h, s, hd)
    kk = _aug_keys(kk, jnp.ones((b, h, s, 1), BF16))
    vt = _aug_values_t(kv5[:, :, :, 1].transpose(0, 3, 1, 4, 2))
    o = _moba_attn(_pad_to(qt, 2, AUG), _pad_to(km, 3, AUG), kk, vt).transpose(0, 3, 1, 2).reshape(b * s, MIX_WIDTH)
    y_p = _out_proj(o, z, xp2, w_out, _row_tile(b * s)).reshape(b, s, d)

    xs2 = x_s.reshape(db * ds, d)
    qs, kvs, zs = project(xs2, jnp.tile(p_len + jnp.arange(ds, dtype=jnp.int32), db))
    rows = ds * h
    q_rows = (qs * SCALE).reshape(db, rows, 1, hd)
    head_of_row = jnp.arange(rows) % h
    diag = (head_of_row[:, None] == jnp.arange(h)[None, :]).astype(F32)
    qbd = (q_rows * diag[None, :, :, None]).astype(BF16).reshape(db, rows, MIX_WIDTH)
    kvs3 = kvs.astype(BF16).reshape(db, ds, 2 * MIX_WIDTH)
    k_new = _pad_rows(kvs3[:, :, :MIX_WIDTH], NEW_PAD)
    v_new = _pad_rows(kvs3[:, :, MIX_WIDTH:], NEW_PAD)
    cache_view = jnp.moveaxis(cache, 1, -1).reshape(cache.shape[0], 2 * MIX_WIDTH, PAGE_SIZE)
    o_full = _moba_sample(page_table, cache_view, qbd, k_new, v_new, ds)
    o6 = o_full.reshape(db, ds, h, h, hd)
    o_s = o6[:, :, jnp.arange(h), jnp.arange(h), :].reshape(db * ds, MIX_WIDTH)
    y_s = _out_proj(o_s, zs, xs2, w_out, db * ds).reshape(db, ds, d)
    return (y_p, y_s, kv.reshape(b, s, 2, h, hd), kvs.reshape(db, ds, 2, h, hd))


def _nsa_layer(x_p, x_s, cache, win_past, page_table, norm_g, w_in, q_g, k_g, cmp_pe, cmp_w1, cmp_b1, cmp_w2, w_out):
    b, s, d = x_p.shape
    db, ds, _ = x_s.shape
    h, hd, g, r = N_HEADS, HEAD_DIM, NSA_KV_GROUPS, NSA_REP
    n_pages = page_table.shape[1]
    p_len = n_pages * PAGE_SIZE
    wb = win_past.shape[1]
    tq = WINDOW // 2
    assert s % tq == 0 and s % CMP_STRIDE == 0 and ds <= SEL_BLOCK and p_len % SEL_BLOCK == 0

    o_q, o_rows, o_win = MIX_WIDTH, MIX_WIDTH + 4 * KV_WIDTH, MIX_WIDTH + 6 * KV_WIDTH
    o_z = o_win + 3 * h
    wq, wrows, wwin = w_in[:, :o_q], w_in[:, o_q:o_rows], w_in[:, o_rows:o_win]
    wgt = jnp.pad(w_in[:, o_win:o_z], ((0, 0), (0, LANES - 3 * h)))
    wz = w_in[:, o_z:]
    ones_kv = jnp.ones((KV_WIDTH,), F32)
    gq = jnp.tile(q_g, h)
    grows = jnp.concatenate([ones_kv, ones_kv, jnp.tile(k_g[1], g), ones_kv])
    gwin = jnp.concatenate([jnp.tile(k_g[2], g), ones_kv])

    def project(x2d, pos):
        tabs = _rope_tables(pos)
        tm = _row_tile(x2d.shape[0], PROJ_ROW_TILE)
        tabs = tabs if pos.shape[0] % tm == 0 else tuple(jnp.tile(t, (tm // pos.shape[0], 1)) for t in tabs)
        q = _proj(x2d, norm_g, wq, [NORM_ROPE] * 4, gq, tabs, tm)
        rows_ = _proj(x2d, norm_g, wrows, [ROPE, PLAIN, NORM_ROPE, PLAIN], grows, tabs, tm)
        win_ = _proj(x2d, norm_g, wwin, [NORM_ROPE, PLAIN], gwin, tabs, tm)
        gt_ = _proj(x2d, norm_g, wgt, [PLAIN], jnp.ones((LANES,), F32), tabs, tm, tn=LANES)
        z = _proj(x2d, norm_g, wz, [PLAIN] * 4, jnp.ones((MIX_WIDTH,), F32), tabs, tm)
        return q, rows_, win_, gt_, z

    pe_flat = cmp_pe.reshape(2, 2, CMP_STRIDE * hd)

    xp2 = x_p.reshape(b * s, d)
    q, rows_p, win_p, gt, z = project(xp2, jnp.arange(s, dtype=jnp.int32))
    nr = s // CMP_STRIDE
    n_cmp = (s - CMP_LEN) // CMP_STRIDE + 1
    n_sel = -(-s // SEL_BLOCK)
    nsp = -(-n_sel // LANES) * LANES
    rows6 = rows_p.reshape(b, nr, CMP_STRIDE, 4, g, hd)
    x16 = rows6[:, :, :, :2].transpose(3, 0, 4, 1, 2, 5).reshape(2, b * g, nr, CMP_STRIDE * hd)
    cmp_kv = _compress(x16, pe_flat, cmp_w1, cmp_b1, cmp_w2, k_g[0]).astype(BF16).reshape(2, b, g, nr, hd)
    kc = _pad_to(cmp_kv[0], 3, AUG)
    vct = cmp_kv[1].transpose(0, 1, 3, 2)
    nq = s // tq
    rows_bf = rows_p.astype(BF16).reshape(b, nq, tq, 4, g, hd)
    block_in_tile = jax.nn.one_hot(jnp.arange(tq) // SEL_BLOCK, tq // SEL_BLOCK, dtype=BF16)
    ks = _aug_keys(rows_bf[:, :, :, 2].transpose(0, 3, 1, 2, 4),
                   jnp.broadcast_to(block_in_tile, (b, g, nq, tq, tq // SEL_BLOCK)))
    vst = _aug_values_t(rows_bf[:, :, :, 3].transpose(0, 3, 1, 4, 2))
    win_bf = win_p.astype(BF16).reshape(b, nq, tq, 2, g, hd)
    kw = _pad_to(win_bf[:, :, :, 0].transpose(0, 3, 1, 2, 4), 4, AUG)
    vwt = _aug_values_t(win_bf[:, :, :, 1].transpose(0, 3, 1, 4, 2))
    qt = (q * (SCALE * LOG2E)).astype(BF16).reshape(b, nq, tq, g, r, hd).transpose(0, 3, 1, 5, 4, 2)
    qt = _pad_to(qt.reshape(b, g, nq, hd, r * tq), 3, AUG)
    gt_g = gt[:, :3 * h].reshape(b, s, g, 3 * r).transpose(0, 2, 3, 1)
    gt_g = jnp.pad(gt_g, ((0, 0), (0, 0), (0, 16 - 3 * r), (0, 0)))
    ovt = _overlap(nr, n_cmp, nsp, n_sel).T
    o = _nsa_attn(qt, kc, vct, ks, vst, kw, vwt, gt_g, ovt, n_cmp)
    o = o.transpose(0, 4, 1, 2, 3).reshape(b * s, MIX_WIDTH)
    y_p = _out_proj(o, z, xp2, w_out, _row_tile(b * s)).reshape(b, s, d)
    win_len = min(WINDOW, s)
    win_out_p = win_p.reshape(b, s, 2, g, hd)[:, s - win_len:]

    xs2 = x_s.reshape(db * ds, d)
    qs, rows_s, win_s, gts, zs = project(xs2, jnp.tile(p_len + jnp.arange(ds, dtype=jnp.int32), db))
    t_all = p_len + ds
    nr_s = p_len // CMP_STRIDE
    n_cmp_s = (t_all - CMP_LEN) // CMP_STRIDE + 1
    assert n_cmp_s <= nr_s - 1
    n_sel_s = -(-t_all // SEL_BLOCK)
    nsp_s = -(-n_sel_s // LANES) * LANES
    cache_view = jnp.moveaxis(cache, 1, -1).reshape(cache.shape[0], 4 * KV_WIDTH, PAGE_SIZE)
    x16_s = _page_rows16(page_table, cache_view)
    cmp_s = _compress(x16_s, pe_flat, cmp_w1, cmp_b1, cmp_w2, k_g[0]).astype(BF16).reshape(2, db, g, nr_s, hd)
    cmp_s = cmp_s.transpose(0, 1, 3, 2, 4).reshape(2, db, nr_s, KV_WIDTH)
    n_rows = r * ds * g
    q_rows = (qs * SCALE).reshape(db, ds, g, r, hd).transpose(0, 3, 1, 2, 4).reshape(db, n_rows, 1, hd)
    group_of_row = jnp.arange(n_rows) % g
    diag = (group_of_row[:, None] == jnp.arange(g)[None, :]).astype(F32)
    qbd = (q_rows * diag[None, :, :, None]).astype(BF16).reshape(db, n_rows, KV_WIDTH)
    rows_s3 = rows_s.astype(BF16).reshape(db, ds, 4, KV_WIDTH)
    win_s3 = win_s.astype(BF16).reshape(db, ds, 2, KV_WIDTH)
    new = lambda a: _pad_rows(a, NEW_PAD)
    gts_r = gts[:, :3 * h].reshape(db, ds, g, r, 3).transpose(0, 3, 1, 2, 4).reshape(db, n_rows, 3)
    gts_r = jnp.pad(gts_r, ((0, 0), (0, 0), (0, LANES - 3)))
    ov_s = _overlap(nr_s, n_cmp_s, nsp_s, n_sel_s)
    win_view = jnp.moveaxis(win_past, 1, -1).reshape(db, 2 * KV_WIDTH, wb)
    o_full = _nsa_sample(page_table, cache_view, qbd, cmp_s[0], cmp_s[1], win_view,
                         new(rows_s3[:, :, 2]), new(rows_s3[:, :, 3]), new(win_s3[:, :, 0]), new(win_s3[:, :, 1]),
                         gts_r, ov_s, ds, p_len, n_cmp_s)
    o6 = o_full.reshape(db, r, ds, g, g, hd)
    o_s = o6[:, :, :, jnp.arange(g), jnp.arange(g), :]
    o_s = o_s.transpose(0, 2, 3, 1, 4).reshape(db * ds, MIX_WIDTH)
    y_s = _out_proj(o_s, zs, xs2, w_out, db * ds).reshape(db, ds, d)
    win_new = win_s.reshape(db, ds, 2, g, hd)
    win_out_s = jnp.concatenate([win_past, win_new], axis=1)[:, ds:]
    return (y_p, y_s, rows_p.reshape(b, s, 4, g, hd), rows_s.reshape(db, ds, 4, g, hd), win_out_p, win_out_s)


def kernel(x_prompt, x_sample, cache_moba_kv, cache_nsa_kv, state_nsa_win, page_table, a_norm, a_w_in, a_q_norm,
           a_k_norm, a_w_out, b_norm, b_w_in, b_q_norm, b_k_norm, b_cmp_pe, b_cmp_w1, b_cmp_b1, b_cmp_w2, b_w_out):
    depth = a_norm.shape[0] + b_norm.shape[0]
    n_pool = cache_moba_kv.shape[1]
    moba_pool = cache_moba_kv.reshape((-1,) + cache_moba_kv.shape[2:])
    nsa_pool = cache_nsa_kv.reshape((-1,) + cache_nsa_kv.shape[2:])
    xp, xs = x_prompt, x_sample
    moba_p, moba_s, nsa_p, nsa_s, win_p, win_s = [], [], [], [], [], []
    for layer in range(depth):
        i = layer // 2
        pt = page_table + i * n_pool
        if layer % 2 == 0:
            xp, xs, kvp, kvs = _moba_layer(xp, xs, moba_pool, pt, a_norm[i], a_w_in[i], a_q_norm[i],
                                           a_k_norm[i], a_w_out[i])
            moba_p.append(kvp)
            moba_s.append(kvs)
        else:
            xp, xs, rp, rs, wp, ws = _nsa_layer(xp, xs, nsa_pool, state_nsa_win[i], pt, b_norm[i],
                                                b_w_in[i], b_q_norm[i], b_k_norm[i], b_cmp_pe[i], b_cmp_w1[i],
                                                b_cmp_b1[i], b_cmp_w2[i], b_w_out[i])
            nsa_p.append(rp)
            nsa_s.append(rs)
            win_p.append(wp)
            win_s.append(ws)
    return (xp, xs, jnp.stack(moba_p), jnp.stack(moba_s), jnp.stack(nsa_p), jnp.stack(nsa_s),
            jnp.stack(win_p), jnp.stack(win_s))
```
